```python
import functools
import jax, jax.numpy as jnp
from jax import lax
import numpy as np

D_MODEL = 1024
BATCH = 2
SEQ = 8192
DEPTH = 2
DEC_BATCH = 32
DEC_SEQ = 4
PAST_LEN = 16384
PAGE_SIZE = 128

N_HEADS = 8
HEAD_DIM = 64
IDX_HEADS = 4
IDX_DIM = 64
TOPK_MAX = 256
Q_BLOCK = 128
RET_HEADS = 4
RET_DK = 128
RET_DV = 128
RET_CHUNK = 128
N_GROUPS = 4
EXPERTS_PER_GROUP = 4
N_EXPERTS = N_GROUPS * EXPERTS_PER_GROUP
D_EXPERT = 256
TOP_IN_GROUP = 2
ROPE_THETA = 10000.0
NORM_EPS = 1e-6
ATTN_W = N_HEADS * HEAD_DIM
RET_QK_W = RET_HEADS * RET_DK
RET_V_W = RET_HEADS * RET_DV
SPLITS = (ATTN_W, ATTN_W, ATTN_W, IDX_HEADS * IDX_DIM, IDX_DIM, IDX_HEADS,
          RET_QK_W, RET_QK_W, RET_V_W, RET_V_W, D_MODEL, D_MODEL)
D_IN = sum(SPLITS)

kernel_name = 'dsa_retention_hmoe_adaln_step'


def rms_norm(x, g):
    xf = x.astype(jnp.float32)
    y = xf * lax.rsqrt(jnp.mean(xf * xf, axis=-1, keepdims=True) + NORM_EPS)
    return (y * g.astype(jnp.float32)).astype(x.dtype)


def modulate(h, shift, scale):
    return h * (1.0 + scale[:, None, :]) + shift[:, None, :]


def rope(x, pos):
    d = x.shape[-1]
    inv = ROPE_THETA ** (-jnp.arange(0, d, 2, dtype=jnp.float32) / d)
    ang = pos.astype(jnp.float32)[:, None] * inv[None, :]
    cos = jnp.cos(ang)[:, None, :]
    sin = jnp.sin(ang)[:, None, :]
    xf = x.astype(jnp.float32)
    x1, x2 = xf[..., : d // 2], xf[..., d // 2:]
    return jnp.concatenate([x1 * cos - x2 * sin, x2 * cos + x1 * sin], axis=-1).astype(x.dtype)


def gather_rows(x, idx):
    return jax.vmap(lambda xb, ib: xb[ib])(x, idx)


def select_keys(qi, wi, ki, q_pos, key_pos, topk):
    s = jnp.einsum('bqhd,bld->bqhl', qi.astype(jnp.float32), ki.astype(jnp.float32)) * (IDX_DIM ** -0.5)
    score = jnp.einsum('bqhl,bqh->bql', jax.nn.relu(s), wi.astype(jnp.float32) * (IDX_HEADS ** -0.5))
    causal = key_pos[None, :] <= q_pos[:, None]
    score = jnp.where(causal[None], score, -jnp.inf)
    _, idx = lax.top_k(score, topk)
    valid = idx <= q_pos[None, :, None]
    return idx, valid


def sparse_softmax(q, ks, vs, valid):
    s = jnp.einsum('bqhd,bqkhd->bqhk', q.astype(jnp.float32), ks.astype(jnp.float32)) * (HEAD_DIM ** -0.5)
    s = jnp.where(valid[:, :, None, :], s, -jnp.inf)
    p = jax.nn.softmax(s, axis=-1)
    return jnp.einsum('bqhk,bqkhd->bqhd', p, vs.astype(jnp.float32)).astype(q.dtype)


def prompt_attend(q, k, v, qi, wi, ki):
    B, S = q.shape[0], q.shape[1]
    topk = min(TOPK_MAX, S // 4)
    key_pos = jnp.arange(S)

    def block(i):
        start = i * Q_BLOCK
        sl = lambda a: lax.dynamic_slice_in_dim(a, start, Q_BLOCK, axis=1)
        q_pos = start + jnp.arange(Q_BLOCK)
        idx, valid = select_keys(sl(qi), sl(wi), ki, q_pos, key_pos, topk)
        return sparse_softmax(sl(q), gather_rows(k, idx), gather_rows(v, idx), valid)

    out = lax.map(block, jnp.arange(S // Q_BLOCK))
    return jnp.moveaxis(out, 0, 1).reshape(B, S, N_HEADS, HEAD_DIM)


def sample_attend(q, k, v, qi, wi, ki, layer, cache_k, cache_v, cache_kidx, page_table):
    Bd, T = q.shape[0], q.shape[1]
    past = page_table.shape[1] * PAGE_SIZE
    L = past + T
    topk = min(TOPK_MAX, L // 4)
    ki_past = cache_kidx[layer, page_table].reshape(Bd, past, IDX_DIM).astype(ki.dtype)
    ki_all = jnp.concatenate([ki_past, ki], axis=1)
    q_pos = past + jnp.arange(T)
    idx, valid = select_keys(qi, wi, ki_all, q_pos, jnp.arange(L), topk)
    in_past = (idx < past)[..., None, None]
    pidx = jnp.minimum(idx, past - 1)
    phys = jax.vmap(lambda pt, i: pt[i])(page_table, pidx // PAGE_SIZE)
    off = pidx % PAGE_SIZE
    nidx = jnp.clip(idx - past, 0, T - 1)
    k_sel = jnp.where(in_past, cache_k[layer, phys, off].astype(k.dtype), gather_rows(k, nidx))
    v_sel = jnp.where(in_past, cache_v[layer, phys, off].astype(v.dtype), gather_rows(v, nidx))
    return sparse_softmax(q, k_sel, v_sel, valid)


def retention(q, k, v, state0):
    B, T, H, dk = q.shape
    dv = v.shape[-1]
    chunk = RET_CHUNK if T % RET_CHUNK == 0 else T
    n = T // chunk
    log_g = jnp.log(1.0 - 2.0 ** (-5.0 - jnp.arange(H, dtype=jnp.float32)))
    t = jnp.arange(chunk, dtype=jnp.float32)
    diff = t[:, None] - t[None, :]
    decay = jnp.where(diff >= 0, jnp.exp(log_g[:, None, None] * jnp.maximum(diff, 0.0)), 0.0)
    q_dec = jnp.exp(log_g[None, :] * (t[:, None] + 1.0))[None, :, :, None]
    k_dec = jnp.exp(log_g[None, :] * (chunk - 1.0 - t[:, None]))[None, :, :, None]
    c_dec = jnp.exp(log_g * chunk)[None, :, None, None]

    def to_chunks(a):
        return jnp.moveaxis(a.astype(jnp.float32).reshape(B, n, chunk, H, a.shape[-1]), 1, 0)

    def step(s, inp):
        qc, kc, vc = inp
        att = jnp.einsum('bihd,bjhd->bhij', qc, kc) * decay[None]
        o = jnp.einsum('bhij,bjhe->bihe', att, vc) + jnp.einsum('bihd,bhde->bihe', qc, s) * q_dec
        s = s * c_dec + jnp.einsum('bjhd,bjhe->bhde', kc * k_dec, vc)
        return s, o

    s_fin, o = lax.scan(step, state0.astype(jnp.float32), (to_chunks(q), to_chunks(k), to_chunks(v)))
    return jnp.moveaxis(o, 0, 1).reshape(B, T, H, dv).astype(q.dtype), s_fin


def hier_moe(h, w_group, b_group, w_router, b_router, w_e1, w_e3, w_e2):
    n = h.shape[0]
    g_logits = (h @ w_group + b_group).astype(jnp.float32)
    g_idx = jnp.argmax(g_logits, axis=-1)
    g_w = jnp.max(jax.nn.softmax(g_logits, axis=-1), axis=-1, keepdims=True)
    e_logits = (h @ w_router + b_router).astype(jnp.float32).reshape(n, N_GROUPS, EXPERTS_PER_GROUP)
    e_in = e_logits[jnp.arange(n), g_idx]
    top_v, top_i = lax.top_k(jax.nn.softmax(e_in, axis=-1), TOP_IN_GROUP)
    top_v = top_v / jnp.sum(top_v, axis=-1, keepdims=True)
    expert_id = g_idx[:, None] * EXPERTS_PER_GROUP + top_i
    comb = jnp.sum(jax.nn.one_hot(expert_id, N_EXPERTS, dtype=jnp.float32) * (g_w * top_v)[..., None], axis=1)
    a = jnp.einsum('nd,edf->nef', h, w_e1)
    b = jnp.einsum('nd,edf->nef', h, w_e3)
    act = jax.nn.silu(a) * b * comb[..., None].astype(h.dtype)
    return jnp.einsum('nef,efd->nd', act, w_e2)


def layer_forward(x, c, pos, ret_state0, attend,
                  w_ada, b_ada, g_mix, g_ffn, w_in, g_q, g_k, g_kidx, g_ret,
                  w_attn_out, w_ret_out, w_o, w_group, b_group, w_router, b_router,
                  w_e1, w_e3, w_e2):
    B, T, D = x.shape
    ada = jnp.einsum('bd,de->be', jax.nn.silu(c), w_ada) + b_ada
    sh_m, sc_m, gt_m, sh_f, sc_f, gt_f = jnp.split(ada, 6, axis=-1)
    h = modulate(rms_norm(x, g_mix), sh_m, sc_m)
    z = jnp.einsum('btd,de->bte', h, w_in)
    offsets = [int(o) for o in np.cumsum(SPLITS)[:-1]]
    q, k, v, qi, ki, wi, rq, rk, rv, rg, ga, gr = jnp.split(z, offsets, axis=-1)
    q = rope(rms_norm(q.reshape(B, T, N_HEADS, HEAD_DIM), g_q), pos)
    k = rope(rms_norm(k.reshape(B, T, N_HEADS, HEAD_DIM), g_k), pos)
    v = v.reshape(B, T, N_HEADS, HEAD_DIM)
    qi = rope(qi.reshape(B, T, IDX_HEADS, IDX_DIM), pos)
    ki = rope(rms_norm(ki, g_kidx)[:, :, None, :], pos)[:, :, 0, :]
    attn = attend(q, k, v, qi, wi, ki)
    rq = rope(rq.reshape(B, T, RET_HEADS, RET_DK), pos)
    rk = rope(rk.reshape(B, T, RET_HEADS, RET_DK), pos) * (RET_DK ** -0.5)
    rv = rv.reshape(B, T, RET_HEADS, RET_DV)
    ret, ret_state = retention(rq, rk, rv, ret_state0)
    ret = rms_norm(ret, g_ret.reshape(RET_HEADS, RET_DV)).reshape(B, T, RET_V_W) * jax.nn.silu(rg)
    y_a = attn.reshape(B, T, ATTN_W) @ w_attn_out
    y_r = ret @ w_ret_out
    mix = (jax.nn.sigmoid(ga) * y_a + jax.nn.sigmoid(gr) * y_r) @ w_o
    x = x + gt_m[:, None, :] * mix
    h = modulate(rms_norm(x, g_ffn), sh_f, sc_f)
    f = hier_moe(h.reshape(B * T, D), w_group, b_group, w_router, b_router, w_e1, w_e3, w_e2).reshape(B, T, D)
    x = x + gt_f[:, None, :] * f
    return x, k, v, ki, ret_state


def setup_inputs(seed: int = 0) -> dict:
    key = jax.random.key(seed)
    ks = jax.random.split(key, 30)

    def nrm(k, shape, scale=1.0):
        return jax.random.normal(k, shape, jnp.float32) * scale

    n_pages = PAST_LEN // PAGE_SIZE
    used = DEC_BATCH * n_pages
    n_phys = used + max(1, used // 4)
    page_table = jax.random.permutation(ks[6], n_phys)[:used].reshape(DEC_BATCH, n_pages).astype(jnp.int32)
    D = D_MODEL
    return {
        'x_prompt': nrm(ks[0], (BATCH, SEQ, D)),
        'x_sample': nrm(ks[1], (DEC_BATCH, DEC_SEQ, D)),
        'cache_k': nrm(ks[2], (DEPTH, n_phys, PAGE_SIZE, N_HEADS, HEAD_DIM)),
        'cache_v': nrm(ks[3], (DEPTH, n_phys, PAGE_SIZE, N_HEADS, HEAD_DIM)),
        'cache_kidx': nrm(ks[4], (DEPTH, n_phys, PAGE_SIZE, IDX_DIM)),
        'state_ret': nrm(ks[5], (DEPTH, DEC_BATCH, RET_HEADS, RET_DK, RET_DV), 0.5),
        'page_table': page_table,
        'c_prompt': nrm(ks[7], (BATCH, D)),
        'c_sample': nrm(ks[8], (DEC_BATCH, D)),
        'w_ada': nrm(ks[9], (DEPTH, D, 6 * D), 0.5 * D ** -0.5),
        'b_ada': nrm(ks[10], (DEPTH, 6 * D), 0.02),
        'g_mix': 1.0 + nrm(ks[11], (DEPTH, D), 0.05),
        'g_ffn': 1.0 + nrm(ks[12], (DEPTH, D), 0.05),
        'w_in': nrm(ks[13], (DEPTH, D, D_IN), D ** -0.5),
        'g_q': 1.0 + nrm(ks[14], (DEPTH, HEAD_DIM), 0.05),
        'g_k': 1.0 + nrm(ks[15], (DEPTH, HEAD_DIM), 0.05),
        'g_kidx': 1.0 + nrm(ks[16], (DEPTH, IDX_DIM), 0.05),
        'g_ret': 1.0 + nrm(ks[17], (DEPTH, RET_V_W), 0.05),
        'w_attn_out': nrm(ks[18], (DEPTH, ATTN_W, D), ATTN_W ** -0.5),
        'w_ret_out': nrm(ks[19], (DEPTH, RET_V_W, D), RET_V_W ** -0.5),
        'w_o': nrm(ks[20], (DEPTH, D, D), D ** -0.5),
        'w_group': nrm(ks[21], (DEPTH, D, N_GROUPS), D ** -0.5),
        'b_group': nrm(ks[22], (DEPTH, N_GROUPS), 0.01),
        'w_router': nrm(ks[23], (DEPTH, D, N_EXPERTS), D ** -0.5),
        'b_router': nrm(ks[24], (DEPTH, N_EXPERTS), 0.01),
        'w_e1': nrm(ks[25], (DEPTH, N_EXPERTS, D, D_EXPERT), D ** -0.5),
        'w_e3': nrm(ks[26], (DEPTH, N_EXPERTS, D, D_EXPERT), D ** -0.5),
        'w_e2': nrm(ks[27], (DEPTH, N_EXPERTS, D_EXPERT, D), D_EXPERT ** -0.5),
    }


def reference(x_prompt, x_sample, cache_k, cache_v, cache_kidx, state_ret, page_table, c_prompt, c_sample,
              w_ada, b_ada, g_mix, g_ffn, w_in, g_q, g_k, g_kidx, g_ret, w_attn_out, w_ret_out, w_o,
              w_group, b_group, w_router, b_router, w_e1, w_e3, w_e2):
    past = page_table.shape[1] * PAGE_SIZE
    pos_p = jnp.arange(x_prompt.shape[1])
    pos_s = past + jnp.arange(x_sample.shape[1])
    y_p, y_s = x_prompt, x_sample
    kp, vp, kip, sp = [], [], [], []
    ksm, vsm, kism, ssm = [], [], [], []
    for l in range(DEPTH):
        lw = (w_ada[l], b_ada[l], g_mix[l], g_ffn[l], w_in[l], g_q[l], g_k[l], g_kidx[l], g_ret[l],
              w_attn_out[l], w_ret_out[l], w_o[l], w_group[l], b_group[l], w_router[l], b_router[l],
              w_e1[l], w_e3[l], w_e2[l])
        s0 = jnp.zeros((x_prompt.shape[0], RET_HEADS, RET_DK, RET_DV), jnp.float32)
        y_p, k_l, v_l, ki_l, s_l = layer_forward(y_p, c_prompt, pos_p, s0, prompt_attend, *lw)
        kp.append(k_l)
        vp.append(v_l)
        kip.append(ki_l)
        sp.append(s_l)
        attend_s = functools.partial(sample_attend, layer=l, cache_k=cache_k, cache_v=cache_v,
                                     cache_kidx=cache_kidx, page_table=page_table)
        y_s, k_l, v_l, ki_l, s_l = layer_forward(y_s, c_sample, pos_s, state_ret[l], attend_s, *lw)
        ksm.append(k_l)
        vsm.append(v_l)
        kism.append(ki_l)
        ssm.append(s_l)
    new_k_prompt = jnp.stack(kp)
    new_v_prompt = jnp.stack(vp)
    new_kidx_prompt = jnp.stack(kip)
    new_state_ret_prompt = jnp.stack(sp)
    new_k_sample = jnp.stack(ksm)
    new_v_sample = jnp.stack(vsm)
    new_kidx_sample = jnp.stack(kism)
    new_state_ret_sample = jnp.stack(ssm)
    return (y_p, y_s, new_k_prompt, new_v_prompt, new_kidx_prompt, new_state_ret_prompt,
            new_k_sample, new_v_sample, new_kidx_sample, new_state_ret_sample)
```

```python
import functools
import math

import numpy as np
import jax
import jax.numpy as jnp
from jax import lax
from jax.experimental import pallas as pl
from jax.experimental.pallas import tpu as pltpu

N_HEADS = 8
HEAD_DIM = 64
IDX_HEADS = 4
IDX_DIM = 64
TOPK_MAX = 256
PAGE_SIZE = 128
RET_HEADS = 4
RET_DK = 128
RET_DV = 128
RET_CHUNK = 128
N_GROUPS = 4
EXPERTS_PER_GROUP = 4
N_EXPERTS = N_GROUPS * EXPERTS_PER_GROUP
D_EXPERT = 256
ROPE_THETA = 10000.0
NORM_EPS = 1e-6
ATTN_W = N_HEADS * HEAD_DIM
RET_W = RET_HEADS * RET_DK

LANES = 128
VMEM_LIMIT = 56 * 1024 * 1024
NEG_BIG = -1e30
INT_MIN = -2 ** 31

F32 = jnp.float32
BF16 = jnp.bfloat16
I32 = jnp.int32


def _dot(a, b):
    return jnp.dot(a, b, preferred_element_type=F32)


def _dot_nt(a, b):
    return lax.dot_general(a, b, (((1,), (1,)), ((), ())), preferred_element_type=F32)


def _dot_tn(a, b):
    return lax.dot_general(a, b, (((0,), (0,)), ((), ())), preferred_element_type=F32)


def _split3(a):
    a1 = a.astype(BF16)
    r1 = a - a1.astype(F32)
    a2 = r1.astype(BF16)
    a3 = (r1 - a2.astype(F32)).astype(BF16)
    return a1, a2, a3


def _dot3(a3, b3, dot=_dot, terms=6):
    a1, a2, a_3 = a3
    b1, b2, b_3 = b3
    mid = dot(a1, b2) + dot(a2, b1)
    if terms == 6:
        mid = mid + (dot(a1, b_3) + dot(a_3, b1) + dot(a2, b2))
    return dot(a1, b1) + mid


def _norm_mod(x, g, shift, scale):
    ms = jnp.mean(x * x, axis=-1, keepdims=True)
    y = x * lax.rsqrt(ms + NORM_EPS) * g
    return y * (1.0 + scale) + shift


def _lane_iota(shape):
    return lax.broadcasted_iota(I32, shape, len(shape) - 1)


def _rope(x, cos, sin, half):
    if 2 * half == LANES:
        swapped = pltpu.roll(x, half, 1)
    else:
        first = (_lane_iota(x.shape) & (2 * half - 1)) < half
        swapped = jnp.where(first, pltpu.roll(x, LANES - half, 1), pltpu.roll(x, half, 1))
    return x * cos + swapped * sin


def _group_ones(width):
    r = lax.broadcasted_iota(I32, (LANES, LANES), 0) // width
    c = lax.broadcasted_iota(I32, (LANES, LANES), 1) // width
    return jnp.where(r == c, 1.0, 0.0).astype(BF16)


def _group_mean_sq(x, ones, width):
    y1, y2, y3 = _split3(x * x)
    return (_dot(y1, ones) + (_dot(y2, ones) + _dot(y3, ones))) * (1.0 / width)


def _sortable(x):
    bits = lax.bitcast_convert_type(x, I32)
    return bits ^ ((bits >> 31) & jnp.int32(0x7FFFFFFF))


def _ada_kernel(c_ref, w_ref, b_ref, o_ref):
    c = c_ref[...]
    s = c * jax.nn.sigmoid(c)
    o_ref[...] = _dot3(_split3(s), _split3(w_ref[...])) + b_ref[...]


def _ada(c_all, w_ada, b_ada):
    m, d = c_all.shape
    n = w_ada.shape[1]
    tn = 1024
    return pl.pallas_call(
        _ada_kernel,
        grid=(n // tn,),
        in_specs=[pl.BlockSpec((m, d), lambda j: (0, 0)),
                  pl.BlockSpec((d, tn), lambda j: (0, j)),
                  pl.BlockSpec((1, tn), lambda j: (0, j))],
        out_specs=pl.BlockSpec((m, tn), lambda j: (0, j)),
        out_shape=jax.ShapeDtypeStruct((m, n), F32),
        compiler_params=pltpu.CompilerParams(vmem_limit_bytes=VMEM_LIMIT),
        name="ada",
    )(c_all, w_ada, b_ada.reshape(1, n))


_OFF_Q, _OFF_K, _OFF_V, _OFF_RQ, _OFF_RK, _OFF_RV, _OFF_RG = (0, 512, 1024, 1536, 2048, 2560, 3072)
_W_MAIN = 3584
_W_IDX = 512


def _proj_kernel(x_ref, g_ref, sh_ref, sc_ref, c64_ref, s64_ref, c128_ref, s128_ref,
                 wm_ref, wi_ref, gq_ref, gk_ref, gki_ref,
                 qb_ref, k_ref, kb_ref, v_ref, vb_ref, qi_ref, ki_ref, wo_ref,
                 rq_ref, rk_ref, rv_ref, rg_ref):
    h = _norm_mod(x_ref[...], g_ref[...], sh_ref[...], sc_ref[...])
    hb = h.astype(BF16)
    c64, s64 = c64_ref[...], s64_ref[...]
    c128, s128 = c128_ref[...], s128_ref[...]
    ones64 = _group_ones(HEAD_DIM)
    gq, gk = gq_ref[...], gk_ref[...]

    def seg(off, j):
        return _dot(hb, wm_ref[:, off + j * LANES: off + (j + 1) * LANES])

    for j in range(ATTN_W // LANES):
        sl = slice(j * LANES, (j + 1) * LANES)
        zq = seg(_OFF_Q, j)
        q = zq * lax.rsqrt(_group_mean_sq(zq, ones64, HEAD_DIM) + NORM_EPS) * gq
        qb_ref[:, sl] = (_rope(q, c64, s64, HEAD_DIM // 2) * (HEAD_DIM ** -0.5)).astype(BF16)
        zk = seg(_OFF_K, j)
        k = zk * lax.rsqrt(_group_mean_sq(zk, ones64, HEAD_DIM) + NORM_EPS) * gk
        k = _rope(k, c64, s64, HEAD_DIM // 2)
        k_ref[:, sl] = k
        kb_ref[:, sl] = k.astype(BF16)
        v = seg(_OFF_V, j)
        v_ref[:, sl] = v
        vb_ref[:, sl] = v.astype(BF16)
    for j in range(RET_W // LANES):
        sl = slice(j * LANES, (j + 1) * LANES)
        rq_ref[:, sl] = _rope(seg(_OFF_RQ, j), c128, s128, RET_DK // 2).astype(BF16)
        rk_ref[:, sl] = (_rope(seg(_OFF_RK, j), c128, s128, RET_DK // 2) * (RET_DK ** -0.5)).astype(BF16)
        rv_ref[:, sl] = seg(_OFF_RV, j).astype(BF16)
        rg_ref[:, sl] = seg(_OFF_RG, j)

    zi = _dot3(_split3(h), _split3(wi_ref[...]))
    for j in range(2):
        sl = slice(j * LANES, (j + 1) * LANES)
        qi_ref[:, sl] = _rope(zi[:, sl], c64, s64, IDX_DIM // 2)
    zk = zi[:, 2 * LANES: 3 * LANES]
    ms = jnp.sum(zk * zk, axis=-1, keepdims=True) * (1.0 / IDX_DIM)
    ki = _rope(zk * lax.rsqrt(ms + NORM_EPS) * gki_ref[...], c64, s64, IDX_DIM // 2)
    ki_ref[...] = ki[:, :IDX_DIM]
    wo_ref[...] = zi[:, 3 * LANES: 4 * LANES]


def _proj(x, g, shift, scale, tabs, wm, wi, gq, gk, gki, tm):
    n, d = x.shape
    nt = n // tm
    groups, r, _ = shift.shape
    per = nt // groups if r == 1 else nt
    row = lambda w: pl.BlockSpec((tm, w), lambda i: (i, 0))
    full = lambda a: pl.BlockSpec(a.shape, lambda i: (0,) * a.ndim)
    mod = pl.BlockSpec((None, r, d), lambda i: (i // per, 0, 0))
    c64, s64, c128, s128 = tabs
    outs = [(ATTN_W, BF16), (ATTN_W, F32), (ATTN_W, BF16), (ATTN_W, F32), (ATTN_W, BF16),
            (IDX_HEADS * IDX_DIM, F32), (IDX_DIM, F32), (LANES, F32),
            (RET_W, BF16), (RET_W, BF16), (RET_W, BF16), (RET_W, F32)]
    return pl.pallas_call(
        _proj_kernel,
        grid=(nt,),
        in_specs=[row(d), full(g), mod, mod, row(LANES), row(LANES), row(LANES), row(LANES),
                  full(wm), full(wi), full(gq), full(gk), full(gki)],
        out_specs=[row(w) for w, _ in outs],
        out_shape=[jax.ShapeDtypeStruct((n, w), dt) for w, dt in outs],
        compiler_params=pltpu.CompilerParams(vmem_limit_bytes=VMEM_LIMIT),
        name="proj",
    )(x, g, shift, scale, c64, s64, c128, s128, wm, wi, gq, gk, gki)


def _select_bias(s_ref, b_ref, n_chunks, kc, topk, idx_bits, row_pos, key_base):
    rows = s_ref.shape[0]

    def count(pred):
        def body(c, cnt):
            off = pl.multiple_of(c * kc, kc)
            m = jnp.where(pred(s_ref[:, pl.ds(off, kc)], off), 1.0, 0.0)
            for u in range(kc // LANES):
                cnt = cnt + m[:, u * LANES:(u + 1) * LANES]
            return cnt
        cnt = lax.fori_loop(0, n_chunks, body, jnp.zeros((rows, LANES), F32))
        return jnp.sum(cnt, axis=-1, keepdims=True)

    def value_bit(it, p):
        cand = p | lax.shift_left(jnp.int32(1), 31 - it)
        t = cand ^ jnp.int32(INT_MIN)
        c = count(lambda s, off: s >= t)
        return jnp.where(c >= topk, cand, p)

    p = lax.fori_loop(0, 32, value_bit, jnp.zeros((rows, 1), I32))
    t_star = p ^ jnp.int32(INT_MIN)
    need = topk - count(lambda s, off: s > t_star)

    def index_bit(it, r):
        cand = r | lax.shift_left(jnp.int32(1), idx_bits - 1 - it)

        def pred(s, off):
            idx = off + _lane_iota((rows, kc))
            return jnp.where(idx < cand, s, t_star + 1) == t_star
        c = count(pred)
        return jnp.where(c < need, cand, r)

    j_star = lax.fori_loop(0, idx_bits, index_bit, jnp.zeros((rows, 1), I32))

    def write(c, carry):
        off = pl.multiple_of(c * kc, kc)
        s = s_ref[:, pl.ds(off, kc)]
        idx = off + _lane_iota((rows, kc))
        tie = jnp.where(idx <= j_star, s, t_star + 1) == t_star
        bias = jnp.where(s > t_star, 0.0, jnp.where(tie, 0.0, NEG_BIG))
        b_ref[:, pl.ds(off, kc)] = jnp.where(key_base + idx <= row_pos, bias, NEG_BIG)
        return carry

    lax.fori_loop(0, n_chunks, write, 0)


def _index_scores(q3, w, k3):
    acc = None
    for hd in range(IDX_HEADS):
        s = _dot3(q3[hd], k3, dot=_dot_nt, terms=3)
        term = jnp.maximum(s, 0.0) * w[:, hd:hd + 1]
        acc = term if acc is None else acc + term
    return acc + 0.0


def _prompt_attend_kernel(qb_ref, qi_ref, w_ref, ki_ref, kb_ref, vb_ref, o_ref, s_ref, b_ref,
                          *, tq, kc, topk, idx_bits):
    i = pl.program_id(1)
    q0 = i * tq
    n_chunks = (q0 + tq + kc - 1) // kc
    row_pos = q0 + lax.broadcasted_iota(I32, (tq, 1), 0)

    qi = qi_ref[...] * (IDX_DIM ** -0.5)
    q3 = [_split3(qi[:, hd * IDX_DIM:(hd + 1) * IDX_DIM]) for hd in range(IDX_HEADS)]
    w = w_ref[...] * (IDX_HEADS ** -0.5)

    def score_chunk(c, carry):
        off = pl.multiple_of(c * kc, kc)
        sc = _index_scores(q3, w, _split3(ki_ref[pl.ds(off, kc), :]))
        key_pos = off + _lane_iota((tq, kc))
        s_ref[:, pl.ds(off, kc)] = jnp.where(key_pos <= row_pos, _sortable(sc), jnp.int32(INT_MIN))
        return carry

    lax.fori_loop(0, n_chunks, score_chunk, 0)
    _select_bias(s_ref, b_ref, n_chunks, kc, topk, idx_bits, row_pos, 0)

    lane = _lane_iota((tq, LANES))
    for j in range(ATTN_W // LANES):
        sl = slice(j * LANES, (j + 1) * LANES)
        qp = qb_ref[:, sl]
        halves = []
        for lo in (True, False):
            qh = jnp.where((lane < HEAD_DIM) == lo, qp, jnp.zeros_like(qp))

            def body(c, carry, qh=qh, sl=sl):
                m, l, acc = carry
                off = pl.multiple_of(c * kc, kc)
                s = _dot_nt(qh, kb_ref[pl.ds(off, kc), sl]) + b_ref[:, pl.ds(off, kc)]
                mn = jnp.maximum(m, jnp.max(s, axis=-1, keepdims=True))
                a = jnp.exp(m - mn)
                p = jnp.exp(s - mn)
                l = a * l + jnp.sum(p, axis=-1, keepdims=True)
                acc = a * acc + _dot(p.astype(BF16), vb_ref[pl.ds(off, kc), sl])
                return mn, l, acc

            init = (jnp.full((tq, 1), NEG_BIG, F32), jnp.zeros((tq, 1), F32), jnp.zeros((tq, LANES), F32))
            m, l, acc = lax.fori_loop(0, n_chunks, body, init)
            halves.append(acc / l)
        o_ref[:, sl] = jnp.where(lane < HEAD_DIM, halves[0], halves[1])


def _prompt_attend(qb, qi, wi, ki, kb, vb, batch, seq):
    tq, kc = 128, min(512, seq)
    topk = min(TOPK_MAX, seq // 4)
    nq = seq // tq
    idx_bits = max(1, (seq - 1).bit_length())
    blk = lambda w: pl.BlockSpec((tq, w), lambda b, i: (b * nq + i, 0))
    res = lambda w: pl.BlockSpec((seq, w), lambda b, i: (b, 0), pipeline_mode=pl.Buffered(1))
    return pl.pallas_call(
        functools.partial(_prompt_attend_kernel, tq=tq, kc=kc, topk=topk, idx_bits=idx_bits),
        grid=(batch, nq),
        in_specs=[blk(ATTN_W), blk(IDX_HEADS * IDX_DIM), blk(LANES), res(IDX_DIM), res(ATTN_W), res(ATTN_W)],
        out_specs=blk(ATTN_W),
        out_shape=jax.ShapeDtypeStruct((batch * seq, ATTN_W), F32),
        scratch_shapes=[pltpu.VMEM((tq, seq), I32), pltpu.VMEM((tq, seq), F32)],
        compiler_params=pltpu.CompilerParams(vmem_limit_bytes=VMEM_LIMIT,
                                             dimension_semantics=("arbitrary", "arbitrary")),
        name="prompt_attend",
    )(qb, qi, wi, ki, kb, vb)


_IDX_PAGES = 16
_KV_PAGES = 8
_ROWS = 8


def _sample_select_kernel(pt_ref, q_ref, w_ref, kn_ref, *rest, past, tail, topk, idx_bits, tokens):
    pages = rest[:_IDX_PAGES]
    bias_ref, s_ref = rest[_IDX_PAGES:]
    g = pl.program_id(1)
    q = q_ref[...] * (IDX_DIM ** -0.5)
    q3 = _split3(q)
    w = w_ref[...] * (IDX_HEADS ** -0.5)

    def scores(keys):
        s = _dot3(q3, _split3(keys), dot=_dot_nt, terms=3)
        acc = None
        for hd in range(IDX_HEADS):
            term = jnp.maximum(s[hd * _ROWS:(hd + 1) * _ROWS], 0.0) * w[:, hd:hd + 1]
            acc = term if acc is None else acc + term
        return acc + 0.0

    for u in range(_IDX_PAGES):
        off = pl.multiple_of((g * _IDX_PAGES + u) * PAGE_SIZE, PAGE_SIZE)
        s_ref[:, pl.ds(off, PAGE_SIZE)] = _sortable(scores(pages[u][...]))

    @pl.when(g == pl.num_programs(1) - 1)
    def _():
        row_tok = lax.broadcasted_iota(I32, (_ROWS, 1), 0) & (tokens - 1)
        lane = _lane_iota((_ROWS, LANES))
        sn = _sortable(scores(kn_ref[...]))
        s_ref[:, past:past + LANES] = jnp.where(lane <= row_tok, sn, jnp.int32(INT_MIN))
        if tail > LANES:
            s_ref[:, past + LANES:] = jnp.full((_ROWS, tail - LANES), INT_MIN, I32)
        _select_bias(s_ref, bias_ref, (past + tail) // tail, tail, topk, idx_bits, past + row_tok, 0)


def _sample_tail(past):
    return 512 if past % 512 == 0 else LANES


def _sample_select(page_table, layer, cache_kidx, q32, w8, ki_new, tokens):
    bd, n_pages = page_table.shape
    past = n_pages * PAGE_SIZE
    tail = _sample_tail(past)
    length = past + tokens
    topk = min(TOPK_MAX, length // 4)
    idx_bits = (past + tail - 1).bit_length()
    steps = n_pages // _IDX_PAGES
    page = lambda u: pl.BlockSpec((None, None, PAGE_SIZE, IDX_DIM),
                                  lambda b, g, pt: (layer, pt[b, g * _IDX_PAGES + u], 0, 0))
    per_b = lambda r, w: pl.BlockSpec((None, r, w), lambda b, g, pt: (b, 0, 0))
    grid_spec = pltpu.PrefetchScalarGridSpec(
        num_scalar_prefetch=1,
        grid=(bd, steps),
        in_specs=[per_b(IDX_HEADS * _ROWS, IDX_DIM), per_b(_ROWS, LANES), per_b(LANES, IDX_DIM)]
        + [page(u) for u in range(_IDX_PAGES)],
        out_specs=per_b(_ROWS, past + tail),
        scratch_shapes=[pltpu.VMEM((_ROWS, past + tail), I32)],
    )
    return pl.pallas_call(
        functools.partial(_sample_select_kernel, past=past, tail=tail, topk=topk, idx_bits=idx_bits,
                          tokens=tokens),
        grid_spec=grid_spec,
        out_shape=jax.ShapeDtypeStruct((bd, _ROWS, past + tail), F32),
        compiler_params=pltpu.CompilerParams(vmem_limit_bytes=VMEM_LIMIT,
                                             dimension_semantics=("arbitrary", "arbitrary")),
        name="sample_select",
    )(page_table, q32, w8, ki_new, *([cache_kidx] * _IDX_PAGES))


def _sample_attend_kernel(pt_ref, q_ref, bias_ref, kn_ref, vn_ref, *rest, past):
    kpages = rest[:_KV_PAGES]
    vpages = rest[_KV_PAGES:2 * _KV_PAGES]
    o_ref, m_ref, l_ref, acc_ref = rest[2 * _KV_PAGES:]
    g = pl.program_id(1)
    n_pairs = ATTN_W // LANES

    @pl.when(g == 0)
    def _():
        m_ref[...] = jnp.full(m_ref.shape, NEG_BIG, F32)
        l_ref[...] = jnp.zeros(l_ref.shape, F32)
        acc_ref[...] = jnp.zeros(acc_ref.shape, F32)

    row = lax.broadcasted_iota(I32, (_ROWS, LANES), 0)
    lane = _lane_iota((_ROWS, LANES))
    keep = (row < _ROWS // 2) == (lane < HEAD_DIM)

    def attend(keys, values, bias):
        for j in range(n_pairs):
            sl = slice(j * LANES, (j + 1) * LANES)
            qp = q_ref[:, sl]
            qh = jnp.where(keep, qp, jnp.zeros_like(qp))
            s = _dot_nt(qh, keys[:, sl]) + bias
            m = m_ref[j]
            mn = jnp.maximum(m, jnp.max(s, axis=-1, keepdims=True))
            a = jnp.exp(m - mn)
            p = jnp.exp(s - mn)
            l_ref[j] = a * l_ref[j] + jnp.sum(p, axis=-1, keepdims=True)
            acc_ref[j] = a * acc_ref[j] + _dot(p, values[:, sl])
            m_ref[j] = mn

    for u in range(_KV_PAGES):
        off = pl.multiple_of((g * _KV_PAGES + u) * PAGE_SIZE, PAGE_SIZE)
        attend(kpages[u][...], vpages[u][...], bias_ref[:, pl.ds(off, PAGE_SIZE)])

    @pl.when(g == pl.num_programs(1) - 1)
    def _():
        attend(kn_ref[...], vn_ref[...], bias_ref[:, past:past + LANES])
        for j in range(n_pairs):
            o_ref[:, j * LANES:(j + 1) * LANES] = acc_ref[j] / l_ref[j]


def _sample_attend(page_table, layer, cache_k, cache_v, q8, bias, k_new, v_new):
    bd, n_pages = page_table.shape
    past = n_pages * PAGE_SIZE
    steps = n_pages // _KV_PAGES
    n_pairs = ATTN_W // LANES
    page = lambda u: pl.BlockSpec((None, None, PAGE_SIZE, ATTN_W),
                                  lambda b, g, pt: (layer, pt[b, g * _KV_PAGES + u], 0, 0))
    per_b = lambda r, w: pl.BlockSpec((None, r, w), lambda b, g, pt: (b, 0, 0))
    grid_spec = pltpu.PrefetchScalarGridSpec(
        num_scalar_prefetch=1,
        grid=(bd, steps),
        in_specs=[per_b(_ROWS, ATTN_W), per_b(_ROWS, bias.shape[2]), per_b(LANES, ATTN_W), per_b(LANES, ATTN_W)]
        + [page(u) for u in range(_KV_PAGES)] * 2,
        out_specs=per_b(_ROWS, ATTN_W),
        scratch_shapes=[pltpu.VMEM((n_pairs, _ROWS, 1), F32), pltpu.VMEM((n_pairs, _ROWS, 1), F32),
                        pltpu.VMEM((n_pairs, _ROWS, LANES), F32)],
    )
    return pl.pallas_call(
        functools.partial(_sample_attend_kernel, past=past),
        grid_spec=grid_spec,
        out_shape=jax.ShapeDtypeStruct((bd, _ROWS, ATTN_W), F32),
        compiler_params=pltpu.CompilerParams(vmem_limit_bytes=VMEM_LIMIT,
                                             dimension_semantics=("arbitrary", "arbitrary")),
        name="sample_attend",
    )(page_table, q8, bias, k_new, v_new, *([cache_k] * _KV_PAGES), *([cache_v] * _KV_PAGES))


def _ret_kernel(q_ref, k_ref, v_ref, s0_ref, o_ref, sout_ref, state_ref, *, chunk, true_len):
    c = pl.program_id(1)

    @pl.when(c == 0)
    def _():
        state_ref[...] = s0_ref[...]

    ti = lax.broadcasted_iota(I32, (chunk, chunk), 0)
    tj = lax.broadcasted_iota(I32, (chunk, chunk), 1)
    diff = (ti - tj).astype(F32)
    t = lax.broadcasted_iota(I32, (chunk, 1), 0).astype(F32)
    for hd in range(RET_HEADS):
        sl = slice(hd * RET_DK, (hd + 1) * RET_DK)
        log_g = math.log(1.0 - 2.0 ** (-5.0 - hd))
        decay = jnp.where(diff >= 0, jnp.exp(log_g * jnp.maximum(diff, 0.0)), 0.0)
        q_dec = jnp.exp(log_g * (t + 1.0))
        k_dec = jnp.exp(log_g * (true_len - 1.0 - t))
        c_dec = math.exp(log_g * true_len)
        q, k, v = q_ref[:, sl], k_ref[:, sl], v_ref[:, sl]
        s = state_ref[hd]
        att = _dot_nt(q, k) * decay
        o_ref[:, sl] = _dot(att.astype(BF16), v) + _dot(q, s.astype(BF16)) * q_dec
        kd = (k.astype(F32) * k_dec).astype(BF16)
        state_ref[hd] = s * c_dec + _dot_tn(kd, v)

    @pl.when(c == pl.num_programs(1) - 1)
    def _():
        sout_ref[...] = state_ref[...]


def _retention(rq, rk, rv, state0, chunk, true_len):
    b = state0.shape[0]
    n = rq.shape[0] // (b * chunk)
    blk = pl.BlockSpec((chunk, RET_W), lambda i, c: (i * n + c, 0))
    st = pl.BlockSpec((None, RET_HEADS, RET_DK, RET_DV), lambda i, c: (i, 0, 0, 0))
    return pl.pallas_call(
        functools.partial(_ret_kernel, chunk=chunk, true_len=true_len),
        grid=(b, n),
        in_specs=[blk, blk, blk, st],
        out_specs=[blk, st],
        out_shape=[jax.ShapeDtypeStruct(rq.shape, F32), jax.ShapeDtypeStruct(state0.shape, F32)],
        scratch_shapes=[pltpu.VMEM((RET_HEADS, RET_DK, RET_DV), F32)],
        compiler_params=pltpu.CompilerParams(vmem_limit_bytes=VMEM_LIMIT,
                                             dimension_semantics=("arbitrary", "arbitrary")),
        name="retention",
    )(rq, rk, rv, state0)


def _mix_kernel(x_ref, g_ref, sh_ref, sc_ref, gt_ref, attn_ref, ret_ref, rg_ref, gret_ref,
                wga_ref, wgr_ref, wa_ref, wr_ref, wo_ref, o_ref):
    x = x_ref[...]
    hb = _norm_mod(x, g_ref[...], sh_ref[...], sc_ref[...]).astype(BF16)
    ret = ret_ref[...]
    rg = rg_ref[...]
    parts = []
    for hd in range(RET_HEADS):
        sl = slice(hd * RET_DV, (hd + 1) * RET_DV)
        r = ret[:, sl]
        ms = jnp.mean(r * r, axis=-1, keepdims=True)
        gate = rg[:, sl]
        parts.append(r * lax.rsqrt(ms + NORM_EPS) * gret_ref[:, sl] * (gate * jax.nn.sigmoid(gate)))
    retn = jnp.concatenate(parts, axis=-1).astype(BF16)
    y_a = _dot(attn_ref[...].astype(BF16), wa_ref[...])
    y_r = _dot(retn, wr_ref[...])
    ga = jax.nn.sigmoid(_dot(hb, wga_ref[...]))
    gr = jax.nn.sigmoid(_dot(hb, wgr_ref[...]))
    mix = _dot((ga * y_a + gr * y_r).astype(BF16), wo_ref[...])
    o_ref[...] = x + gt_ref[...] * mix


def _mix(x, g, shift, scale, gate, attn, ret, rg, gret, wga, wgr, wa, wr, wo, tm):
    n, d = x.shape
    nt = n // tm
    groups, r, _ = shift.shape
    per = nt // groups if r == 1 else nt
    row = lambda w: pl.BlockSpec((tm, w), lambda i: (i, 0))
    full = lambda a: pl.BlockSpec(a.shape, lambda i: (0,) * a.ndim)
    mod = pl.BlockSpec((None, r, d), lambda i: (i // per, 0, 0))
    return pl.pallas_call(
        _mix_kernel,
        grid=(nt,),
        in_specs=[row(d), full(g), mod, mod, mod, row(ATTN_W), row(RET_W), row(RET_W), full(gret),
                  full(wga), full(wgr), full(wa), full(wr), full(wo)],
        out_specs=row(d),
        out_shape=jax.ShapeDtypeStruct((n, d), F32),
        compiler_params=pltpu.CompilerParams(vmem_limit_bytes=VMEM_LIMIT),
        name="mix",
    )(x, g, shift, scale, gate, attn, ret, rg, gret, wga, wgr, wa, wr, wo)


_ROUTER_GROUP_LANE = N_EXPERTS


def _moe_kernel(x_ref, g_ref, sh_ref, sc_ref, gt_ref, wr_ref, br_ref, w1_ref, w3_ref, w2_ref,
                o_ref, hb_ref, comb_ref, acc_ref):
    e = pl.program_id(1)
    tm = x_ref.shape[0]
    lane = _lane_iota((tm, LANES))

    @pl.when(e == 0)
    def _():
        h = _norm_mod(x_ref[...], g_ref[...], sh_ref[...], sc_ref[...])
        hb_ref[...] = h.astype(BF16)
        logits = _dot3(_split3(h), _split3(wr_ref[...])) + br_ref[...]
        lane_f = lane.astype(F32)

        def first_lane(hit):
            return jnp.min(jnp.where(hit, lane_f, float(LANES)), axis=-1, keepdims=True)

        is_group = jnp.abs(lane_f - (_ROUTER_GROUP_LANE + (N_GROUPS - 1) / 2)) < N_GROUPS / 2
        gl = jnp.where(is_group, logits, -jnp.inf)
        gmax = jnp.max(gl, axis=-1, keepdims=True)
        g_idx = first_lane(gl == gmax) - _ROUTER_GROUP_LANE
        g_w = 1.0 / jnp.sum(jnp.exp(gl - gmax), axis=-1, keepdims=True)
        group_mid = g_idx * EXPERTS_PER_GROUP + (EXPERTS_PER_GROUP - 1) / 2
        el = jnp.where(jnp.abs(lane_f - group_mid) < EXPERTS_PER_GROUP / 2, logits, -jnp.inf)
        m1 = jnp.max(el, axis=-1, keepdims=True)
        i1 = first_lane(el == m1)
        el2 = jnp.where(lane_f == i1, -jnp.inf, el)
        m2 = jnp.max(el2, axis=-1, keepdims=True)
        i2 = first_lane(el2 == m2)
        e2 = jnp.exp(m2 - m1)
        v1 = 1.0 / (1.0 + e2)
        v2 = e2 / (1.0 + e2)
        comb_ref[...] = jnp.where(lane_f == i1, g_w * v1, jnp.where(lane_f == i2, g_w * v2, 0.0))
        acc_ref[...] = jnp.zeros(acc_ref.shape, F32)

    hb = hb_ref[...]
    a = _dot(hb, w1_ref[...])
    b = _dot(hb, w3_ref[...])
    ce = jnp.sum(jnp.where(lane == e, comb_ref[...], 0.0), axis=-1, keepdims=True)
    act = (a * jax.nn.sigmoid(a)) * b * ce
    acc_ref[...] += _dot(act.astype(BF16), w2_ref[...])

    @pl.when(e == pl.num_programs(1) - 1)
    def _():
        o_ref[...] = x_ref[...] + gt_ref[...] * acc_ref[...]


def _moe(x, g, shift, scale, gate, w_route, b_route, w1, w3, w2, tm):
    n, d = x.shape
    nt = n // tm
    groups, r, _ = shift.shape
    per = nt // groups if r == 1 else nt
    row = pl.BlockSpec((tm, d), lambda i, e: (i, 0))
    full = lambda a: pl.BlockSpec(a.shape, lambda i, e: (0,) * a.ndim)
    mod = pl.BlockSpec((None, r, d), lambda i, e: (i // per, 0, 0))
    return pl.pallas_call(
        _moe_kernel,
        grid=(nt, N_EXPERTS),
        in_specs=[row, full(g), mod, mod, mod, full(w_route), full(b_route),
                  pl.BlockSpec((None, d, D_EXPERT), lambda i, e: (e, 0, 0)),
                  pl.BlockSpec((None, d, D_EXPERT), lambda i, e: (e, 0, 0)),
                  pl.BlockSpec((None, D_EXPERT, d), lambda i, e: (e, 0, 0))],
        out_specs=row,
        out_shape=jax.ShapeDtypeStruct((n, d), F32),
        scratch_shapes=[pltpu.VMEM((tm, d), BF16), pltpu.VMEM((tm, LANES), F32), pltpu.VMEM((tm, d), F32)],
        compiler_params=pltpu.CompilerParams(vmem_limit_bytes=VMEM_LIMIT,
                                             dimension_semantics=("arbitrary", "arbitrary")),
        name="moe",
    )(x, g, shift, scale, gate, w_route, b_route, w1, w3, w2)


def _rope_tables(pos, dim):
    inv = ROPE_THETA ** (-jnp.arange(0, dim, 2, dtype=F32) / dim)
    ang = pos.astype(F32)[:, None] * inv[None, :]
    cos = jnp.concatenate([jnp.cos(ang), jnp.cos(ang)], axis=-1)
    sin = jnp.concatenate([-jnp.sin(ang), jnp.sin(ang)], axis=-1)
    rep = LANES // dim
    return jnp.tile(cos, (1, rep)), jnp.tile(sin, (1, rep))


def _pack_weights(l, w_in, g_q, g_k, g_kidx, w_group, b_group, w_router, b_router):
    offs = np.cumsum((ATTN_W, ATTN_W, ATTN_W, IDX_HEADS * IDX_DIM, IDX_DIM, IDX_HEADS,
                      RET_W, RET_W, RET_W, RET_W))
    w = w_in[l]
    d = w.shape[0]
    q, k, v, qi, ki, wi, rq, rk, rv, rg = (w[:, a:b] for a, b in zip((0, *offs[:-1]), offs))
    w_gate = w[:, offs[-1]:]
    wm = jnp.concatenate([q, k, v, rq, rk, rv, rg], axis=1).astype(BF16)
    zeros = lambda n: jnp.zeros((d, n), F32)
    widx = jnp.concatenate([qi, ki, zeros(LANES - IDX_DIM), wi, zeros(LANES - IDX_HEADS)], axis=1)
    half = w_gate.shape[1] // 2
    wga, wgr = w_gate[:, :half].astype(BF16), w_gate[:, half:].astype(BF16)
    gq = jnp.tile(g_q[l], LANES // HEAD_DIM)[None, :]
    gk = jnp.tile(g_k[l], LANES // HEAD_DIM)[None, :]
    gki = jnp.concatenate([g_kidx[l], jnp.zeros((LANES - IDX_DIM,), F32)])[None, :]
    pad = LANES - N_EXPERTS - N_GROUPS
    w_route = jnp.concatenate([w_router[l], w_group[l], zeros(pad)], axis=1)
    b_route = jnp.concatenate([b_router[l], b_group[l], jnp.zeros((pad,), F32)])[None, :]
    return wm, widx, wga, wgr, gq, gk, gki, w_route, b_route


def kernel(x_prompt, x_sample, cache_k, cache_v, cache_kidx, state_ret, page_table, c_prompt, c_sample,
           w_ada, b_ada, g_mix, g_ffn, w_in, g_q, g_k, g_kidx, g_ret, w_attn_out, w_ret_out, w_o,
           w_group, b_group, w_router, b_router, w_e1, w_e3, w_e2):
    bp, seq, d = x_prompt.shape
    bd, tokens, _ = x_sample.shape
    depth = w_in.shape[0]
    n_pages = page_table.shape[1]
    past = n_pages * PAGE_SIZE
    n_p, n_s = bp * seq, bd * tokens
    assert tokens * 2 == _ROWS and n_pages % _IDX_PAGES == 0 and seq % RET_CHUNK == 0

    tabs_p = _rope_tables(jnp.arange(seq), HEAD_DIM) + _rope_tables(jnp.arange(seq), RET_DK)
    tabs_p = tuple(jnp.tile(t, (bp, 1)) for t in tabs_p)
    pos_s = past + jnp.arange(tokens)
    tabs_s = _rope_tables(pos_s, HEAD_DIM) + _rope_tables(pos_s, RET_DK)
    tabs_s = tuple(jnp.tile(t, (bd, 1)) for t in tabs_s)

    c_all = jnp.concatenate([c_prompt, c_sample], axis=0)
    c_rows = -(-c_all.shape[0] // 8) * 8
    c_all = jnp.pad(c_all, ((0, c_rows - c_all.shape[0]), (0, 0)))
    ck = cache_k.reshape(cache_k.shape[:3] + (ATTN_W,))
    cv = cache_v.reshape(cache_v.shape[:3] + (ATTN_W,))
    ret_pad = 16
    tm_p = 256 if n_p % 256 == 0 else 128
    tm_moe = 1024 if n_p % 1024 == 0 else tm_p

    y_p = x_prompt.reshape(n_p, d)
    y_s = x_sample.reshape(n_s, d)
    outs = {name: [] for name in ("kp", "vp", "kip", "sp", "ks", "vs", "kis", "ss")}
    for l in range(depth):
        wm, widx, wga, wgr, gq, gk, gki, w_route, b_route = _pack_weights(
            l, w_in, g_q, g_k, g_kidx, w_group, b_group, w_router, b_router)
        wa, wr, wo = w_attn_out[l].astype(BF16), w_ret_out[l].astype(BF16), w_o[l].astype(BF16)
        w1, w3, w2 = w_e1[l].astype(BF16), w_e3[l].astype(BF16), w_e2[l].astype(BF16)
        gmix, gffn, gret = g_mix[l][None, :], g_ffn[l][None, :], g_ret[l][None, :]

        ada = _ada(c_all, w_ada[l], b_ada[l])
        mods_p = [m[:bp, None, :] for m in jnp.split(ada, 6, axis=-1)]
        mods_s = [jnp.repeat(m[bp:bp + bd], tokens, axis=0)[None] for m in jnp.split(ada, 6, axis=-1)]

        sh_m, sc_m, gt_m, sh_f, sc_f, gt_f = mods_p
        (qb, k, kb, v, vb, qi, ki, wi, rq, rk, rv, rg) = _proj(
            y_p, gmix, sh_m, sc_m, tabs_p, wm, widx, gq, gk, gki, tm_p)
        attn = _prompt_attend(qb, qi, wi, ki, kb, vb, bp, seq)
        ret, s_fin = _retention(rq, rk, rv, jnp.zeros((bp, RET_HEADS, RET_DK, RET_DV), F32),
                                RET_CHUNK, RET_CHUNK)
        y_p = _mix(y_p, gmix, sh_m, sc_m, gt_m, attn, ret, rg, gret, wga, wgr, wa, wr, wo, tm_p)
        y_p = _moe(y_p, gffn, sh_f, sc_f, gt_f, w_route, b_route, w1, w3, w2, tm_moe)
        outs["kp"].append(k.reshape(bp, seq, N_HEADS, HEAD_DIM))
        outs["vp"].append(v.reshape(bp, seq, N_HEADS, HEAD_DIM))
        outs["kip"].append(ki.reshape(bp, seq, IDX_DIM))
        outs["sp"].append(s_fin)

        sh_m, sc_m, gt_m, sh_f, sc_f, gt_f = mods_s
        (qb, k, kb, v, vb, qi, ki, wi, rq, rk, rv, rg) = _proj(
            y_s, gmix, sh_m, sc_m, tabs_s, wm, widx, gq, gk, gki, n_s)
        twice = lambda a: jnp.concatenate([a, a], axis=1)
        q32 = twice(qi.reshape(bd, tokens, IDX_HEADS, IDX_DIM)).transpose(0, 2, 1, 3)
        q32 = q32.reshape(bd, IDX_HEADS * _ROWS, IDX_DIM)
        w8 = twice(wi.reshape(bd, tokens, LANES))
        pad_rows = lambda a: jnp.pad(a.reshape(bd, tokens, -1), ((0, 0), (0, LANES - tokens), (0, 0)))
        bias = _sample_select(page_table, l, cache_kidx, q32, w8, pad_rows(ki), tokens)
        q8 = twice(qb.reshape(bd, tokens, ATTN_W)).astype(F32)
        o8 = _sample_attend(page_table, l, ck, cv, q8, bias, pad_rows(k), pad_rows(v))
        first_head = (jnp.arange(ATTN_W) % LANES) < HEAD_DIM
        attn = jnp.where(first_head, o8[:, :tokens], o8[:, tokens:]).reshape(n_s, ATTN_W)
        pad_chunk = lambda a: jnp.pad(a.reshape(bd, tokens, RET_W),
                                      ((0, 0), (0, ret_pad - tokens), (0, 0))).reshape(bd * ret_pad, RET_W)
        ret, s_fin = _retention(pad_chunk(rq), pad_chunk(rk), pad_chunk(rv), state_ret[l], ret_pad, tokens)
        ret = ret.reshape(bd, ret_pad, RET_W)[:, :tokens].reshape(n_s, RET_W)
        y_s = _mix(y_s, gmix, sh_m, sc_m, gt_m, attn, ret, rg, gret, wga, wgr, wa, wr, wo, n_s)
        y_s = _moe(y_s, gffn, sh_f, sc_f, gt_f, w_route, b_route, w1, w3, w2, n_s)
        outs["ks"].append(k.reshape(bd, tokens, N_HEADS, HEAD_DIM))
        outs["vs"].append(v.reshape(bd, tokens, N_HEADS, HEAD_DIM))
        outs["kis"].append(ki.reshape(bd, tokens, IDX_DIM))
        outs["ss"].append(s_fin)

    st = lambda name: jnp.stack(outs[name])
    return (y_p.reshape(bp, seq, d), y_s.reshape(bd, tokens, d),
            st("kp"), st("vp"), st("kip"), st("sp"), st("ks"), st("vs"), st("kis"), st("ss"))
```

```python
import functools
import math

import numpy as np
import jax
import jax.numpy as jnp
from jax import lax
from jax.experimental import pallas as pl
from jax.experimental.pallas import tpu as pltpu

N_HEADS = 8
HEAD_DIM = 64
IDX_HEADS = 4
IDX_DIM = 64
TOPK_MAX = 256
PAGE_SIZE = 128
RET_HEADS = 4
RET_DK = 128
RET_DV = 128
RET_CHUNK = 128
N_GROUPS = 4
EXPERTS_PER_GROUP = 4
N_EXPERTS = N_GROUPS * EXPERTS_PER_GROUP
D_EXPERT = 256
ROPE_THETA = 10000.0
NORM_EPS = 1e-6
ATTN_W = N_HEADS * HEAD_DIM
RET_W = RET_HEADS * RET_DK

LANES = 128
VMEM_LIMIT = 56 * 1024 * 1024
NEG_BIG = -1e30
_Q_SCALE = HEAD_DIM ** -0.5 * math.log2(math.e)
INT_MIN = -2 ** 31

F32 = jnp.float32
BF16 = jnp.bfloat16
I32 = jnp.int32


def _dot(a, b):
    return jnp.dot(a, b, preferred_element_type=F32)


def _dot_nt(a, b):
    return lax.dot_general(a, b, (((1,), (1,)), ((), ())), preferred_element_type=F32)


def _dot_tn(a, b):
    return lax.dot_general(a, b, (((0,), (0,)), ((), ())), preferred_element_type=F32)


def _split3(a):
    a1 = a.astype(BF16)
    r1 = a - a1.astype(F32)
    a2 = r1.astype(BF16)
    a3 = (r1 - a2.astype(F32)).astype(BF16)
    return a1, a2, a3


def _dot3(a3, b3, dot=_dot, terms=6):
    a1, a2, a_3 = a3
    b1, b2, b_3 = b3
    mid = dot(a1, b2) + dot(a2, b1)
    if terms == 6:
        mid = mid + (dot(a1, b_3) + dot(a_3, b1) + dot(a2, b2))
    return dot(a1, b1) + mid


def _norm_mod(x, g, shift, scale):
    ms = jnp.mean(x * x, axis=-1, keepdims=True)
    y = x * lax.rsqrt(ms + NORM_EPS) * g
    return y * (1.0 + scale) + shift


def _lane_iota(shape):
    return lax.broadcasted_iota(I32, shape, len(shape) - 1)


def _rope(x, cos, sin, half):
    if 2 * half == LANES:
        swapped = pltpu.roll(x, half, 1)
    else:
        first = (_lane_iota(x.shape) & (2 * half - 1)) < half
        swapped = jnp.where(first, pltpu.roll(x, LANES - half, 1), pltpu.roll(x, half, 1))
    return x * cos + swapped * sin


def _group_ones(width):
    r = lax.broadcasted_iota(I32, (LANES, LANES), 0) // width
    c = lax.broadcasted_iota(I32, (LANES, LANES), 1) // width
    return jnp.where(r == c, 1.0, 0.0).astype(BF16)


def _group_mean_sq(x, ones, width):
    y1, y2, y3 = _split3(x * x)
    return (_dot(y1, ones) + (_dot(y2, ones) + _dot(y3, ones))) * (1.0 / width)


def _sortable(x):
    bits = lax.bitcast_convert_type(x, I32)
    return bits ^ ((bits >> 31) & jnp.int32(0x7FFFFFFF))


def _ada_kernel(c_ref, w_ref, b_ref, o_ref):
    c = c_ref[...]
    s = c * jax.nn.sigmoid(c)
    o_ref[...] = _dot3(_split3(s), _split3(w_ref[...])) + b_ref[...]


def _ada(c_all, w_ada, b_ada):
    m, d = c_all.shape
    n = w_ada.shape[1]
    tn = 1024
    return pl.pallas_call(
        _ada_kernel,
        grid=(n // tn,),
        in_specs=[pl.BlockSpec((m, d), lambda j: (0, 0)),
                  pl.BlockSpec((d, tn), lambda j: (0, j)),
                  pl.BlockSpec((1, tn), lambda j: (0, j))],
        out_specs=pl.BlockSpec((m, tn), lambda j: (0, j)),
        out_shape=jax.ShapeDtypeStruct((m, n), F32),
        compiler_params=pltpu.CompilerParams(vmem_limit_bytes=VMEM_LIMIT),
        name="ada",
    )(c_all, w_ada, b_ada.reshape(1, n))


_OFF_Q, _OFF_K, _OFF_V, _OFF_RQ, _OFF_RK, _OFF_RV, _OFF_RG = (0, 512, 1024, 1536, 2048, 2560, 3072)
_W_MAIN = 3584
_W_IDX = 512


def _split_concat(x, hi_half):
    x1, x2, _ = _split3(x)
    x1, x2 = x1.astype(F32), x2.astype(F32)
    lo = _lane_iota(x.shape) < LANES // 2
    r1, r2 = pltpu.roll(x1, LANES // 2, 1), pltpu.roll(x2, LANES // 2, 1)
    if hi_half:
        return jnp.where(lo, r1, x1).astype(BF16), jnp.where(lo, r2, 0.0).astype(BF16)
    return jnp.where(lo, x1, r1).astype(BF16), jnp.where(lo, x2, 0.0).astype(BF16)


def _proj_kernel(x_ref, g_ref, sh_ref, sc_ref, c64_ref, s64_ref, c128_ref, s128_ref,
                 wm_ref, wi_ref, gq_ref, gk_ref, gki_ref,
                 qb_ref, k_ref, kb_ref, v_ref, vb_ref, qi_ref, qcat_ref, ki_ref, kcat_ref, wo_ref,
                 rq_ref, rk_ref, rv_ref, rg_ref):
    h = _norm_mod(x_ref[...], g_ref[...], sh_ref[...], sc_ref[...])
    hb = h.astype(BF16)
    c64, s64 = c64_ref[...], s64_ref[...]
    c128, s128 = c128_ref[...], s128_ref[...]
    ones64 = _group_ones(HEAD_DIM)
    gq, gk = gq_ref[...], gk_ref[...]

    def seg(off, j):
        return _dot(hb, wm_ref[:, off + j * LANES: off + (j + 1) * LANES])

    for j in range(ATTN_W // LANES):
        sl = slice(j * LANES, (j + 1) * LANES)
        zq = seg(_OFF_Q, j)
        q = zq * lax.rsqrt(_group_mean_sq(zq, ones64, HEAD_DIM) + NORM_EPS) * gq
        qb_ref[:, sl] = (_rope(q, c64, s64, HEAD_DIM // 2) * _Q_SCALE).astype(BF16)
        zk = seg(_OFF_K, j)
        k = zk * lax.rsqrt(_group_mean_sq(zk, ones64, HEAD_DIM) + NORM_EPS) * gk
        k = _rope(k, c64, s64, HEAD_DIM // 2)
        k_ref[:, sl] = k
        kb_ref[:, sl] = k.astype(BF16)
        v = seg(_OFF_V, j)
        v_ref[:, sl] = v
        vb_ref[:, sl] = v.astype(BF16)
    for j in range(RET_W // LANES):
        sl = slice(j * LANES, (j + 1) * LANES)
        rq_ref[:, sl] = _rope(seg(_OFF_RQ, j), c128, s128, RET_DK // 2).astype(BF16)
        rk_ref[:, sl] = (_rope(seg(_OFF_RK, j), c128, s128, RET_DK // 2) * (RET_DK ** -0.5)).astype(BF16)
        rv_ref[:, sl] = seg(_OFF_RV, j).astype(BF16)
        rg_ref[:, sl] = seg(_OFF_RG, j)

    zi = _dot3(_split3(h), _split3(wi_ref[...]))
    for j in range(2):
        sl = slice(j * LANES, (j + 1) * LANES)
        qi = _rope(zi[:, sl], c64, s64, IDX_DIM // 2)
        qi_ref[:, sl] = qi
        for hi_half in (False, True):
            hd = 2 * j + hi_half
            a, b = _split_concat(qi * (IDX_DIM ** -0.5), hi_half)
            qcat_ref[:, 2 * hd * LANES:(2 * hd + 1) * LANES] = a
            qcat_ref[:, (2 * hd + 1) * LANES:(2 * hd + 2) * LANES] = b
    zk = zi[:, 2 * LANES: 3 * LANES]
    ms = jnp.sum(zk * zk, axis=-1, keepdims=True) * (1.0 / IDX_DIM)
    ki = _rope(zk * lax.rsqrt(ms + NORM_EPS) * gki_ref[...], c64, s64, IDX_DIM // 2)
    ki_ref[...] = ki[:, :IDX_DIM]
    k1, k2, _ = _split3(ki)
    k1, k2 = k1.astype(F32), k2.astype(F32)
    kcat_ref[:, :LANES] = (k1 + pltpu.roll(k2, LANES // 2, 1)).astype(BF16)
    kcat_ref[:, LANES:] = k1.astype(BF16)
    wo_ref[...] = zi[:, 3 * LANES: 4 * LANES]


def _proj(x, g, shift, scale, tabs, wm, wi, gq, gk, gki, tm):
    n, d = x.shape
    nt = n // tm
    groups, r, _ = shift.shape
    per = nt // groups if r == 1 else nt
    row = lambda w: pl.BlockSpec((tm, w), lambda i: (i, 0))
    full = lambda a: pl.BlockSpec(a.shape, lambda i: (0,) * a.ndim)
    mod = pl.BlockSpec((None, r, d), lambda i: (i // per, 0, 0))
    c64, s64, c128, s128 = tabs
    outs = [(ATTN_W, BF16), (ATTN_W, F32), (ATTN_W, BF16), (ATTN_W, F32), (ATTN_W, BF16),
            (IDX_HEADS * IDX_DIM, F32), (IDX_HEADS * 2 * LANES, BF16), (IDX_DIM, F32), (2 * LANES, BF16),
            (LANES, F32),
            (RET_W, BF16), (RET_W, BF16), (RET_W, BF16), (RET_W, F32)]
    return pl.pallas_call(
        _proj_kernel,
        grid=(nt,),
        in_specs=[row(d), full(g), mod, mod, row(LANES), row(LANES), row(LANES), row(LANES),
                  full(wm), full(wi), full(gq), full(gk), full(gki)],
        out_specs=[row(w) for w, _ in outs],
        out_shape=[jax.ShapeDtypeStruct((n, w), dt) for w, dt in outs],
        compiler_params=pltpu.CompilerParams(vmem_limit_bytes=VMEM_LIMIT),
        name="proj",
    )(x, g, shift, scale, c64, s64, c128, s128, wm, wi, gq, gk, gki)


def _select_bias(s_ref, b_ref, n_chunks, kc, topk, row_pos, key_base):
    rows = s_ref.shape[0]

    def count(pred):
        def body(c, cnt):
            off = pl.multiple_of(c * kc, kc)
            m = jnp.where(pred(s_ref[:, pl.ds(off, kc)], off), 1.0, 0.0)
            for u in range(kc // LANES):
                cnt = cnt + m[:, u * LANES:(u + 1) * LANES]
            return cnt
        cnt = lax.fori_loop(0, n_chunks, body, jnp.zeros((rows, LANES), F32))
        return jnp.sum(cnt, axis=-1, keepdims=True)

    def value_bit(it, p):
        cand = p | lax.shift_left(jnp.int32(1), 31 - it)
        t = cand ^ jnp.int32(INT_MIN)
        c = count(lambda s, off: s >= t)
        return jnp.where(c >= topk, cand, p)

    p = lax.fori_loop(0, 32, value_bit, jnp.zeros((rows, 1), I32))
    t_star = p ^ jnp.int32(INT_MIN)
    need = topk - count(lambda s, off: s > t_star)

    upto = jnp.where(lax.broadcasted_iota(I32, (kc, kc), 0) <= lax.broadcasted_iota(I32, (kc, kc), 1),
                     1.0, 0.0).astype(BF16)

    def write(c, ties_before):
        off = pl.multiple_of(c * kc, kc)
        s = s_ref[:, pl.ds(off, kc)]
        idx = off + _lane_iota((rows, kc))
        tie = jnp.where(s == t_star, 1.0, 0.0)
        rank = ties_before + _dot(tie.astype(BF16), upto)
        tie_kept = jnp.where(rank <= need, tie, 0.0)
        bias = jnp.where(s > t_star, 0.0, jnp.where(tie_kept > 0.0, 0.0, NEG_BIG))
        b_ref[:, pl.ds(off, kc)] = jnp.where(key_base + idx <= row_pos, bias, NEG_BIG)
        return ties_before + jnp.sum(tie, axis=-1, keepdims=True)

    lax.fori_loop(0, n_chunks, write, jnp.zeros((rows, 1), F32))


def _prompt_attend_kernel(qb_ref, qcat_ref, w_ref, kcat_ref, kb_ref, vb_ref, o_ref,
                          s_ref, b_ref, m_ref, l_ref, acc_ref, *, tq, kc, topk):
    i = pl.program_id(1)
    q0 = i * tq
    n_chunks = (q0 + tq + kc - 1) // kc
    row_pos = q0 + lax.broadcasted_iota(I32, (tq, 1), 0)

    width = 2 * LANES
    q_all = jnp.concatenate([qcat_ref[:, hd * width:(hd + 1) * width] for hd in range(IDX_HEADS)], axis=0)
    w = w_ref[...] * (IDX_HEADS ** -0.5)

    def score_chunk(c, carry):
        off = pl.multiple_of(c * kc, kc)
        s = _dot_nt(q_all, kcat_ref[pl.ds(off, kc), :])
        acc = None
        for hd in range(IDX_HEADS):
            term = jnp.maximum(s[hd * tq:(hd + 1) * tq], 0.0) * w[:, hd:hd + 1]
            acc = term if acc is None else acc + term
        key_pos = off + _lane_iota((tq, kc))
        s_ref[:, pl.ds(off, kc)] = jnp.where(key_pos <= row_pos, _sortable(acc + 0.0), jnp.int32(INT_MIN))
        return carry

    lax.fori_loop(0, n_chunks, score_chunk, 0)
    _select_bias(s_ref, b_ref, n_chunks, kc, topk, row_pos, 0)

    m_ref[...] = jnp.full(m_ref.shape, NEG_BIG, F32)
    l_ref[...] = jnp.zeros(l_ref.shape, F32)
    acc_ref[...] = jnp.zeros(acc_ref.shape, F32)
    lane = _lane_iota((tq, LANES))

    def attend_chunk(c, carry):
        off = pl.multiple_of(c * kc, kc)
        bias = b_ref[:, pl.ds(off, kc)]
        for hd in range(N_HEADS):
            sl = slice(hd // 2 * LANES, (hd // 2 + 1) * LANES)
            qp = qb_ref[:, sl]
            qh = jnp.where((lane < HEAD_DIM) == (hd % 2 == 0), qp, jnp.zeros_like(qp))
            s = _dot_nt(qh, kb_ref[pl.ds(off, kc), sl]) + bias
            m_old = m_ref[hd]
            m_new = jnp.maximum(m_old, jnp.max(s, axis=-1, keepdims=True))
            a = jnp.exp2(m_old - m_new)
            p = jnp.exp2(s - m_new)
            l_ref[hd] = a * l_ref[hd] + jnp.sum(p, axis=-1, keepdims=True)
            acc_ref[hd] = a * acc_ref[hd] + _dot(p.astype(BF16), vb_ref[pl.ds(off, kc), sl])
            m_ref[hd] = m_new
        return carry

    lax.fori_loop(0, n_chunks, attend_chunk, 0)
    for j in range(N_HEADS // 2):
        lo = acc_ref[2 * j] / l_ref[2 * j]
        hi = acc_ref[2 * j + 1] / l_ref[2 * j + 1]
        o_ref[:, j * LANES:(j + 1) * LANES] = jnp.where(lane < HEAD_DIM, lo, hi)


def _prompt_attend(qb, qcat, wi, kcat, kb, vb, batch, seq):
    tq, kc = 128, min(512, seq)
    topk = min(TOPK_MAX, seq // 4)
    nq = seq // tq
    blk = lambda w: pl.BlockSpec((tq, w), lambda b, i: (b * nq + i, 0))
    res = lambda w: pl.BlockSpec((seq, w), lambda b, i: (b, 0), pipeline_mode=pl.Buffered(1))
    return pl.pallas_call(
        functools.partial(_prompt_attend_kernel, tq=tq, kc=kc, topk=topk),
        grid=(batch, nq),
        in_specs=[blk(ATTN_W), blk(qcat.shape[1]), blk(LANES), res(kcat.shape[1]), res(ATTN_W), res(ATTN_W)],
        out_specs=blk(ATTN_W),
        out_shape=jax.ShapeDtypeStruct((batch * seq, ATTN_W), F32),
        scratch_shapes=[pltpu.VMEM((tq, seq), I32), pltpu.VMEM((tq, seq), F32),
                        pltpu.VMEM((N_HEADS, tq, 1), F32), pltpu.VMEM((N_HEADS, tq, 1), F32),
                        pltpu.VMEM((N_HEADS, tq, LANES), F32)],
        compiler_params=pltpu.CompilerParams(vmem_limit_bytes=VMEM_LIMIT,
                                             dimension_semantics=("arbitrary", "arbitrary")),
        name="prompt_attend",
    )(qb, qcat, wi, kcat, kb, vb)


_IDX_PAGES = 16
_ROWS = 8
_NEW_KEYS = 16


def _sample_select_kernel(pt_ref, q_ref, w_ref, kn_ref, *rest, n_pages, topk, idx_bits, tokens):
    pages = rest[:_IDX_PAGES]
    idx_ref, meta_ref, s_ref, slot_ref = rest[_IDX_PAGES:]
    g = pl.program_id(1)
    q3 = _split3(q_ref[...] * (IDX_DIM ** -0.5))
    w = w_ref[...] * (IDX_HEADS ** -0.5)

    def scores(keys):
        s = _dot3(q3, _split3(keys), dot=_dot_nt, terms=3)
        acc = None
        for hd in range(IDX_HEADS):
            term = jnp.maximum(s[hd * _ROWS:(hd + 1) * _ROWS], 0.0) * w[:, hd:hd + 1]
            acc = term if acc is None else acc + term
        return _sortable(acc + 0.0)

    page_scores = [scores(pages[u][...]) for u in range(_IDX_PAGES)]
    row0 = pl.multiple_of(g * _IDX_PAGES, _IDX_PAGES)
    for t in range(tokens):
        s_ref[t, pl.ds(row0, _IDX_PAGES), :] = jnp.concatenate([ps[t:t + 1] for ps in page_scores], axis=0)

    @pl.when(g == pl.num_programs(1) - 1)
    def _():
        shape = (tokens, n_pages + _ROWS, LANES)
        lane1 = _lane_iota((1, LANES))
        sn = scores(kn_ref[...])
        floor = jnp.full((_ROWS - 1, LANES), INT_MIN, I32)
        for t in range(tokens):
            new_row = jnp.where(lane1 <= t, sn[t:t + 1], jnp.int32(INT_MIN))
            s_ref[t, n_pages:, :] = jnp.concatenate([new_row, floor], axis=0)

        idx = lax.broadcasted_iota(I32, shape, 1) * LANES + _lane_iota(shape)

        def count(hit):
            c = jnp.sum(jnp.where(hit, 1.0, 0.0), axis=1, keepdims=True)
            return jnp.sum(c, axis=2, keepdims=True)

        def value_bit(it, p):
            cand = p | lax.shift_left(jnp.int32(1), 31 - it)
            c = count(s_ref[...] >= (cand ^ jnp.int32(INT_MIN)))
            return jnp.where(c >= topk, cand, p)

        p = lax.fori_loop(0, 32, value_bit, jnp.zeros((tokens, 1, 1), I32))
        t_star = p ^ jnp.int32(INT_MIN)
        need = topk - count(s_ref[...] > t_star)

        def index_bit(it, r):
            cand = r | lax.shift_left(jnp.int32(1), idx_bits - 1 - it)
            c = count(jnp.where(idx < cand, s_ref[...], jnp.int32(INT_MIN)) == t_star)
            return jnp.where(c < need, cand, r)

        j_star = lax.fori_loop(0, idx_bits, index_bit, jnp.zeros((tokens, 1, 1), I32))
        s = s_ref[...]
        tie = jnp.where(idx <= j_star, s, jnp.int32(INT_MIN)) == t_star
        sel_f = jnp.where(s > t_star, 1.0, jnp.where(tie, 1.0, 0.0))
        sel_f = jnp.where(s == jnp.int32(INT_MIN), 0.0, sel_f)

        r2 = lax.broadcasted_iota(I32, (n_pages, n_pages), 0)
        c2 = lax.broadcasted_iota(I32, (n_pages, n_pages), 1)
        earlier_page = jnp.where(r2 > c2, 1.0, 0.0)
        rl = lax.broadcasted_iota(I32, (LANES, LANES), 0)
        cl = lax.broadcasted_iota(I32, (LANES, LANES), 1)
        upto_lane = jnp.where(rl <= cl, 1.0, 0.0)
        slot_iota = lax.broadcasted_iota(I32, (topk, LANES), 0).astype(F32)
        lane_f = _lane_iota((topk, LANES)).astype(F32)
        meta_rows = []
        for t in range(tokens):
            past_sel = sel_f[t, :n_pages, :]
            per_page = jnp.sum(past_sel, axis=1, keepdims=True)
            n_sel = jnp.sum(per_page, axis=0, keepdims=True)
            new_sel = sel_f[t, n_pages:n_pages + 1, :]
            bit_value = sum(jnp.where(lane1 == j, float(2 ** j), 0.0) for j in range(tokens))
            bits = jnp.sum(new_sel * bit_value, axis=1, keepdims=True)
            meta_rows.append(jnp.where(lane1 == 0, n_sel, jnp.where(lane1 == 1, bits, 0.0)))
            before = _dot(earlier_page, jnp.broadcast_to(per_page, (n_pages, LANES)))
            rank = before + _dot(past_sel, upto_lane) - 1.0
            slot_ref[...] = jnp.where(past_sel > 0.0, rank, -1.0)

            def place(pg, acc):
                hit = slot_ref[pl.ds(pg, 1), :] == slot_iota
                return jnp.where(hit, jnp.asarray(pg * LANES, F32) + lane_f, acc)

            acc = lax.fori_loop(0, n_pages, place, jnp.full((topk, LANES), -1.0, F32))
            idx_ref[t] = jnp.max(acc, axis=1, keepdims=True).astype(I32)
        meta_rows.append(jnp.zeros((_ROWS - tokens, LANES), F32))
        meta_ref[...] = jnp.concatenate(meta_rows, axis=0).astype(I32)


def _sample_select(page_table, layer, cache_kidx, q32, w8, ki_new, tokens):
    bd, n_pages = page_table.shape
    past = n_pages * PAGE_SIZE
    topk = min(TOPK_MAX, (past + tokens) // 4)
    idx_bits = (past + LANES - 1).bit_length()
    steps = n_pages // _IDX_PAGES
    page = lambda u: pl.BlockSpec((None, None, PAGE_SIZE, IDX_DIM),
                                  lambda b, g, pt: (layer, pt[b, g * _IDX_PAGES + u], 0, 0))
    per_b = lambda *blk: pl.BlockSpec((None,) + blk, lambda b, g, pt: (b,) + (0,) * len(blk))
    grid_spec = pltpu.PrefetchScalarGridSpec(
        num_scalar_prefetch=1,
        grid=(bd, steps),
        in_specs=[per_b(IDX_HEADS * _ROWS, IDX_DIM), per_b(_ROWS, LANES), per_b(LANES, IDX_DIM)]
        + [page(u) for u in range(_IDX_PAGES)],
        out_specs=[per_b(tokens, topk, 1), per_b(_ROWS, LANES)],
        scratch_shapes=[pltpu.VMEM((tokens, n_pages + _ROWS, LANES), I32), pltpu.VMEM((n_pages, LANES), F32)],
    )
    return pl.pallas_call(
        functools.partial(_sample_select_kernel, n_pages=n_pages, topk=topk, idx_bits=idx_bits, tokens=tokens),
        grid_spec=grid_spec,
        out_shape=[jax.ShapeDtypeStruct((bd, tokens, topk, 1), I32),
                   jax.ShapeDtypeStruct((bd, _ROWS, LANES), I32)],
        compiler_params=pltpu.CompilerParams(vmem_limit_bytes=VMEM_LIMIT,
                                             dimension_semantics=("arbitrary", "arbitrary")),
        name="sample_select",
    )(page_table, q32, w8, ki_new, *([cache_kidx] * _IDX_PAGES))


def _sample_attend_kernel(pt_ref, idx_ref, nsel_ref, newbits_ref, q_ref, kn_ref, vn_ref, ck_hbm, cv_hbm,
                          o_ref, kbuf, vbuf, sems, *, layer, past, tokens, topk):
    r = pl.program_id(0)
    b = r // tokens

    def row_copy(cache, buf, sem, page, off, slot):
        return pltpu.make_async_copy(cache.at[layer, page, off], buf.at[slot], sem)

    def issue(slot, carry):
        pos = jnp.clip(idx_ref[r, slot], 0, past - 1)
        page = pt_ref[b, pos // PAGE_SIZE]
        off = pos % PAGE_SIZE
        row_copy(ck_hbm, kbuf, sems.at[0], page, off, slot).start()
        row_copy(cv_hbm, vbuf, sems.at[1], page, off, slot).start()
        return carry

    lax.fori_loop(0, topk, issue, 0)

    def drain(slot, carry):
        row_copy(ck_hbm, kbuf, sems.at[0], 0, 0, slot).wait()
        row_copy(cv_hbm, vbuf, sems.at[1], 0, 0, slot).wait()
        return carry

    lax.fori_loop(0, topk, drain, 0)

    q = q_ref[...]
    n_cols = topk * N_HEADS
    keys = kbuf[...].reshape(n_cols, HEAD_DIM)
    vals = vbuf[...].reshape(n_cols, HEAD_DIM)
    col = _lane_iota((N_HEADS, n_cols))
    head = lax.broadcasted_iota(I32, (N_HEADS, n_cols), 0)
    own = jnp.where((col & (N_HEADS - 1)) == head, col, n_cols)
    s_old = jnp.where(own < nsel_ref[r] * N_HEADS, _dot_nt(q, keys), NEG_BIG)

    new_cols = _NEW_KEYS * N_HEADS
    keys_new = kn_ref[...].reshape(new_cols, HEAD_DIM)
    vals_new = vn_ref[...].reshape(new_cols, HEAD_DIM)
    coln = _lane_iota((N_HEADS, new_cols))
    headn = lax.broadcasted_iota(I32, (N_HEADS, new_cols), 0)
    picked = lax.shift_right_logical(jnp.full((N_HEADS, new_cols), newbits_ref[r], I32), coln // N_HEADS) & 1
    ownn = jnp.where((coln & (N_HEADS - 1)) == headn, picked, 0)
    s_new = jnp.where(ownn == 1, _dot_nt(q, keys_new), NEG_BIG)

    m = jnp.maximum(jnp.max(s_old, axis=-1, keepdims=True), jnp.max(s_new, axis=-1, keepdims=True))
    p_old = jnp.exp2(s_old - m)
    p_new = jnp.exp2(s_new - m)
    l = jnp.sum(p_old, axis=-1, keepdims=True) + jnp.sum(p_new, axis=-1, keepdims=True)
    o_ref[...] = (_dot(p_old, vals) + _dot(p_new, vals_new)) / l


def _sample_attend(page_table, idx, n_sel, new_bits, layer, cache_k, cache_v, q, k_new, v_new):
    bd, tokens = q.shape[:2]
    past = page_table.shape[1] * PAGE_SIZE
    topk = idx.shape[1]
    tok_blk = pl.BlockSpec((None, None, N_HEADS, HEAD_DIM), lambda r, *_: (r // tokens, r % tokens, 0, 0))
    new_blk = pl.BlockSpec((None, _NEW_KEYS, N_HEADS, HEAD_DIM), lambda r, *_: (r // tokens, 0, 0, 0))
    grid_spec = pltpu.PrefetchScalarGridSpec(
        num_scalar_prefetch=4,
        grid=(bd * tokens,),
        in_specs=[tok_blk, new_blk, new_blk, pl.BlockSpec(memory_space=pl.ANY), pl.BlockSpec(memory_space=pl.ANY)],
        out_specs=tok_blk,
        scratch_shapes=[pltpu.VMEM((topk, N_HEADS, HEAD_DIM), F32), pltpu.VMEM((topk, N_HEADS, HEAD_DIM), F32),
                        pltpu.SemaphoreType.DMA((2,))],
    )
    return pl.pallas_call(
        functools.partial(_sample_attend_kernel, layer=layer, past=past, tokens=tokens, topk=topk),
        grid_spec=grid_spec,
        out_shape=jax.ShapeDtypeStruct(q.shape, F32),
        compiler_params=pltpu.CompilerParams(vmem_limit_bytes=VMEM_LIMIT, dimension_semantics=("arbitrary",)),
        name="sample_attend",
    )(page_table, idx, n_sel, new_bits, q, k_new, v_new, cache_k, cache_v)


def _ret_kernel(q_ref, k_ref, v_ref, s0_ref, o_ref, sout_ref, state_ref, *, chunk, true_len):
    c = pl.program_id(1)

    @pl.when(c == 0)
    def _():
        state_ref[...] = s0_ref[...]

    ti = lax.broadcasted_iota(I32, (chunk, chunk), 0)
    tj = lax.broadcasted_iota(I32, (chunk, chunk), 1)
    diff = (ti - tj).astype(F32)
    t = lax.broadcasted_iota(I32, (chunk, 1), 0).astype(F32)
    for hd in range(RET_HEADS):
        sl = slice(hd * RET_DK, (hd + 1) * RET_DK)
        log_g = math.log(1.0 - 2.0 ** (-5.0 - hd))
        decay = jnp.where(diff >= 0, jnp.exp(log_g * jnp.maximum(diff, 0.0)), 0.0)
        q_dec = jnp.exp(log_g * (t + 1.0))
        k_dec = jnp.exp(log_g * (true_len - 1.0 - t))
        c_dec = math.exp(log_g * true_len)
        q, k, v = q_ref[:, sl], k_ref[:, sl], v_ref[:, sl]
        s = state_ref[hd]
        att = _dot_nt(q, k) * decay
        o_ref[:, sl] = _dot(att.astype(BF16), v) + _dot(q, s.astype(BF16)) * q_dec
        kd = (k.astype(F32) * k_dec).astype(BF16)
        state_ref[hd] = s * c_dec + _dot_tn(kd, v)

    @pl.when(c == pl.num_programs(1) - 1)
    def _():
        sout_ref[...] = state_ref[...]


def _retention(rq, rk, rv, state0, chunk, true_len):
    b = state0.shape[0]
    n = rq.shape[0] // (b * chunk)
    blk = pl.BlockSpec((chunk, RET_W), lambda i, c: (i * n + c, 0))
    st = pl.BlockSpec((None, RET_HEADS, RET_DK, RET_DV), lambda i, c: (i, 0, 0, 0))
    return pl.pallas_call(
        functools.partial(_ret_kernel, chunk=chunk, true_len=true_len),
        grid=(b, n),
        in_specs=[blk, blk, blk, st],
        out_specs=[blk, st],
        out_shape=[jax.ShapeDtypeStruct(rq.shape, F32), jax.ShapeDtypeStruct(state0.shape, F32)],
        scratch_shapes=[pltpu.VMEM((RET_HEADS, RET_DK, RET_DV), F32)],
        compiler_params=pltpu.CompilerParams(vmem_limit_bytes=VMEM_LIMIT,
                                             dimension_semantics=("arbitrary", "arbitrary")),
        name="retention",
    )(rq, rk, rv, state0)


def _mix_kernel(x_ref, g_ref, sh_ref, sc_ref, gt_ref, attn_ref, ret_ref, rg_ref, gret_ref,
                wga_ref, wgr_ref, wa_ref, wr_ref, wo_ref, o_ref):
    x = x_ref[...]
    hb = _norm_mod(x, g_ref[...], sh_ref[...], sc_ref[...]).astype(BF16)
    ret = ret_ref[...]
    rg = rg_ref[...]
    parts = []
    for hd in range(RET_HEADS):
        sl = slice(hd * RET_DV, (hd + 1) * RET_DV)
        r = ret[:, sl]
        ms = jnp.mean(r * r, axis=-1, keepdims=True)
        gate = rg[:, sl]
        parts.append(r * lax.rsqrt(ms + NORM_EPS) * gret_ref[:, sl] * (gate * jax.nn.sigmoid(gate)))
    retn = jnp.concatenate(parts, axis=-1).astype(BF16)
    y_a = _dot(attn_ref[...].astype(BF16), wa_ref[...])
    y_r = _dot(retn, wr_ref[...])
    ga = jax.nn.sigmoid(_dot(hb, wga_ref[...]))
    gr = jax.nn.sigmoid(_dot(hb, wgr_ref[...]))
    mix = _dot((ga * y_a + gr * y_r).astype(BF16), wo_ref[...])
    o_ref[...] = x + gt_ref[...] * mix


def _mix(x, g, shift, scale, gate, attn, ret, rg, gret, wga, wgr, wa, wr, wo, tm):
    n, d = x.shape
    nt = n // tm
    groups, r, _ = shift.shape
    per = nt // groups if r == 1 else nt
    row = lambda w: pl.BlockSpec((tm, w), lambda i: (i, 0))
    full = lambda a: pl.BlockSpec(a.shape, lambda i: (0,) * a.ndim)
    mod = pl.BlockSpec((None, r, d), lambda i: (i // per, 0, 0))
    return pl.pallas_call(
        _mix_kernel,
        grid=(nt,),
        in_specs=[row(d), full(g), mod, mod, mod, row(ATTN_W), row(RET_W), row(RET_W), full(gret),
                  full(wga), full(wgr), full(wa), full(wr), full(wo)],
        out_specs=row(d),
        out_shape=jax.ShapeDtypeStruct((n, d), F32),
        compiler_params=pltpu.CompilerParams(vmem_limit_bytes=VMEM_LIMIT),
        name="mix",
    )(x, g, shift, scale, gate, attn, ret, rg, gret, wga, wgr, wa, wr, wo)


_ROUTER_GROUP_LANE = N_EXPERTS


def _moe_kernel(x_ref, g_ref, sh_ref, sc_ref, gt_ref, wr_ref, br_ref, w1_ref, w3_ref, w2_ref,
                o_ref, hb_ref, comb_ref, acc_ref):
    e = pl.program_id(1)
    tm = x_ref.shape[0]
    lane = _lane_iota((tm, LANES))

    @pl.when(e == 0)
    def _():
        h = _norm_mod(x_ref[...], g_ref[...], sh_ref[...], sc_ref[...])
        hb_ref[...] = h.astype(BF16)
        logits = _dot3(_split3(h), _split3(wr_ref[...])) + br_ref[...]
        lane_f = lane.astype(F32)

        def first_lane(hit):
            return jnp.min(jnp.where(hit, lane_f, float(LANES)), axis=-1, keepdims=True)

        is_group = jnp.abs(lane_f - (_ROUTER_GROUP_LANE + (N_GROUPS - 1) / 2)) < N_GROUPS / 2
        gl = jnp.where(is_group, logits, -jnp.inf)
        gmax = jnp.max(gl, axis=-1, keepdims=True)
        g_idx = first_lane(gl == gmax) - _ROUTER_GROUP_LANE
        g_w = 1.0 / jnp.sum(jnp.exp(gl - gmax), axis=-1, keepdims=True)
        group_mid = g_idx * EXPERTS_PER_GROUP + (EXPERTS_PER_GROUP - 1) / 2
        el = jnp.where(jnp.abs(lane_f - group_mid) < EXPERTS_PER_GROUP / 2, logits, -jnp.inf)
        m1 = jnp.max(el, axis=-1, keepdims=True)
        i1 = first_lane(el == m1)
        el2 = jnp.where(lane_f == i1, -jnp.inf, el)
        m2 = jnp.max(el2, axis=-1, keepdims=True)
        i2 = first_lane(el2 == m2)
        e2 = jnp.exp(m2 - m1)
        v1 = 1.0 / (1.0 + e2)
        v2 = e2 / (1.0 + e2)
        comb_ref[...] = jnp.where(lane_f == i1, g_w * v1, jnp.where(lane_f == i2, g_w * v2, 0.0))
        acc_ref[...] = jnp.zeros(acc_ref.shape, F32)

    hb = hb_ref[...]
    a = _dot(hb, w1_ref[...])
    b = _dot(hb, w3_ref[...])
    ce = jnp.sum(jnp.where(lane == e, comb_ref[...], 0.0), axis=-1, keepdims=True)
    act = (a * jax.nn.sigmoid(a)) * b * ce
    acc_ref[...] += _dot(act.astype(BF16), w2_ref[...])

    @pl.when(e == pl.num_programs(1) - 1)
    def _():
        o_ref[...] = x_ref[...] + gt_ref[...] * acc_ref[...]


def _moe(x, g, shift, scale, gate, w_route, b_route, w1, w3, w2, tm):
    n, d = x.shape
    nt = n // tm
    groups, r, _ = shift.shape
    per = nt // groups if r == 1 else nt
    row = pl.BlockSpec((tm, d), lambda i, e: (i, 0))
    full = lambda a: pl.BlockSpec(a.shape, lambda i, e: (0,) * a.ndim)
    mod = pl.BlockSpec((None, r, d), lambda i, e: (i // per, 0, 0))
    return pl.pallas_call(
        _moe_kernel,
        grid=(nt, N_EXPERTS),
        in_specs=[row, full(g), mod, mod, mod, full(w_route), full(b_route),
                  pl.BlockSpec((None, d, D_EXPERT), lambda i, e: (e, 0, 0)),
                  pl.BlockSpec((None, d, D_EXPERT), lambda i, e: (e, 0, 0)),
                  pl.BlockSpec((None, D_EXPERT, d), lambda i, e: (e, 0, 0))],
        out_specs=row,
        out_shape=jax.ShapeDtypeStruct((n, d), F32),
        scratch_shapes=[pltpu.VMEM((tm, d), BF16), pltpu.VMEM((tm, LANES), F32), pltpu.VMEM((tm, d), F32)],
        compiler_params=pltpu.CompilerParams(vmem_limit_bytes=VMEM_LIMIT,
                                             dimension_semantics=("arbitrary", "arbitrary")),
        name="moe",
    )(x, g, shift, scale, gate, w_route, b_route, w1, w3, w2)


def _rope_tables(pos, dim):
    inv = ROPE_THETA ** (-jnp.arange(0, dim, 2, dtype=F32) / dim)
    ang = pos.astype(F32)[:, None] * inv[None, :]
    cos = jnp.concatenate([jnp.cos(ang), jnp.cos(ang)], axis=-1)
    sin = jnp.concatenate([-jnp.sin(ang), jnp.sin(ang)], axis=-1)
    rep = LANES // dim
    return jnp.tile(cos, (1, rep)), jnp.tile(sin, (1, rep))


def _pack_weights(l, w_in, g_q, g_k, g_kidx, w_group, b_group, w_router, b_router):
    offs = np.cumsum((ATTN_W, ATTN_W, ATTN_W, IDX_HEADS * IDX_DIM, IDX_DIM, IDX_HEADS,
                      RET_W, RET_W, RET_W, RET_W))
    w = w_in[l]
    d = w.shape[0]
    q, k, v, qi, ki, wi, rq, rk, rv, rg = (w[:, a:b] for a, b in zip((0, *offs[:-1]), offs))
    w_gate = w[:, offs[-1]:]
    wm = jnp.concatenate([q, k, v, rq, rk, rv, rg], axis=1).astype(BF16)
    zeros = lambda n: jnp.zeros((d, n), F32)
    widx = jnp.concatenate([qi, ki, zeros(LANES - IDX_DIM), wi, zeros(LANES - IDX_HEADS)], axis=1)
    half = w_gate.shape[1] // 2
    wga, wgr = w_gate[:, :half].astype(BF16), w_gate[:, half:].astype(BF16)
    gq = jnp.tile(g_q[l], LANES // HEAD_DIM)[None, :]
    gk = jnp.tile(g_k[l], LANES // HEAD_DIM)[None, :]
    gki = jnp.concatenate([g_kidx[l], jnp.zeros((LANES - IDX_DIM,), F32)])[None, :]
    pad = LANES - N_EXPERTS - N_GROUPS
    w_route = jnp.concatenate([w_router[l], w_group[l], zeros(pad)], axis=1)
    b_route = jnp.concatenate([b_router[l], b_group[l], jnp.zeros((pad,), F32)])[None, :]
    return wm, widx, wga, wgr, gq, gk, gki, w_route, b_route


def kernel(x_prompt, x_sample, cache_k, cache_v, cache_kidx, state_ret, page_table, c_prompt, c_sample,
           w_ada, b_ada, g_mix, g_ffn, w_in, g_q, g_k, g_kidx, g_ret, w_attn_out, w_ret_out, w_o,
           w_group, b_group, w_router, b_router, w_e1, w_e3, w_e2):
    bp, seq, d = x_prompt.shape
    bd, tokens, _ = x_sample.shape
    depth = w_in.shape[0]
    n_pages = page_table.shape[1]
    past = n_pages * PAGE_SIZE
    n_p, n_s = bp * seq, bd * tokens
    assert tokens <= _ROWS and n_pages % _IDX_PAGES == 0 and seq % RET_CHUNK == 0

    tabs_p = _rope_tables(jnp.arange(seq), HEAD_DIM) + _rope_tables(jnp.arange(seq), RET_DK)
    tabs_p = tuple(jnp.tile(t, (bp, 1)) for t in tabs_p)
    pos_s = past + jnp.arange(tokens)
    tabs_s = _rope_tables(pos_s, HEAD_DIM) + _rope_tables(pos_s, RET_DK)
    tabs_s = tuple(jnp.tile(t, (bd, 1)) for t in tabs_s)

    c_all = jnp.concatenate([c_prompt, c_sample], axis=0)
    c_rows = -(-c_all.shape[0] // 8) * 8
    c_all = jnp.pad(c_all, ((0, c_rows - c_all.shape[0]), (0, 0)))
    ret_pad = 16
    tm_p = 256 if seq % 256 == 0 else 128
    tm_moe = 1024 if seq % 1024 == 0 else tm_p

    y_p = x_prompt.reshape(n_p, d)
    y_s = x_sample.reshape(n_s, d)
    outs = {name: [] for name in ("kp", "vp", "kip", "sp", "ks", "vs", "kis", "ss")}
    for l in range(depth):
        wm, widx, wga, wgr, gq, gk, gki, w_route, b_route = _pack_weights(
            l, w_in, g_q, g_k, g_kidx, w_group, b_group, w_router, b_router)
        wa, wr, wo = w_attn_out[l].astype(BF16), w_ret_out[l].astype(BF16), w_o[l].astype(BF16)
        w1, w3, w2 = w_e1[l].astype(BF16), w_e3[l].astype(BF16), w_e2[l].astype(BF16)
        gmix, gffn, gret = g_mix[l][None, :], g_ffn[l][None, :], g_ret[l][None, :]

        ada = _ada(c_all, w_ada[l], b_ada[l])
        mods_p = [m[:bp, None, :] for m in jnp.split(ada, 6, axis=-1)]
        mods_s = [jnp.repeat(m[bp:bp + bd], tokens, axis=0)[None] for m in jnp.split(ada, 6, axis=-1)]

        sh_m, sc_m, gt_m, sh_f, sc_f, gt_f = mods_p
        (qb, k, kb, v, vb, qi, qcat, ki, kcat, wi, rq, rk, rv, rg) = _proj(
            y_p, gmix, sh_m, sc_m, tabs_p, wm, widx, gq, gk, gki, tm_p)
        attn = _prompt_attend(qb, qcat, wi, kcat, kb, vb, bp, seq)
        ret, s_fin = _retention(rq, rk, rv, jnp.zeros((bp, RET_HEADS, RET_DK, RET_DV), F32),
                                RET_CHUNK, RET_CHUNK)
        y_p = _mix(y_p, gmix, sh_m, sc_m, gt_m, attn, ret, rg, gret, wga, wgr, wa, wr, wo, tm_p)
        y_p = _moe(y_p, gffn, sh_f, sc_f, gt_f, w_route, b_route, w1, w3, w2, tm_moe)
        outs["kp"].append(k.reshape(bp, seq, N_HEADS, HEAD_DIM))
        outs["vp"].append(v.reshape(bp, seq, N_HEADS, HEAD_DIM))
        outs["kip"].append(ki.reshape(bp, seq, IDX_DIM))
        outs["sp"].append(s_fin)

        sh_m, sc_m, gt_m, sh_f, sc_f, gt_f = mods_s
        (qb, k, kb, v, vb, qi, qcat, ki, kcat, wi, rq, rk, rv, rg) = _proj(
            y_s, gmix, sh_m, sc_m, tabs_s, wm, widx, gq, gk, gki, n_s)
        pad_to = lambda a, rows: jnp.pad(a, ((0, 0), (0, rows - a.shape[1])) + ((0, 0),) * (a.ndim - 2))
        q32 = pad_to(qi.reshape(bd, tokens, IDX_HEADS, IDX_DIM), _ROWS).transpose(0, 2, 1, 3)
        q32 = q32.reshape(bd, IDX_HEADS * _ROWS, IDX_DIM)
        w8 = pad_to(wi.reshape(bd, tokens, LANES), _ROWS)
        idx, meta = _sample_select(page_table, l, cache_kidx, q32, w8,
                                   pad_to(ki.reshape(bd, tokens, IDX_DIM), LANES), tokens)
        heads = lambda a: a.reshape(bd, tokens, N_HEADS, HEAD_DIM)
        attn = _sample_attend(page_table, idx.reshape(n_s, -1), meta[:, :tokens, 0].reshape(n_s),
                              meta[:, :tokens, 1].reshape(n_s), l, cache_k, cache_v,
                              heads(qb.astype(F32)), pad_to(heads(k), _NEW_KEYS), pad_to(heads(v), _NEW_KEYS))
        attn = attn.reshape(n_s, ATTN_W)
        pad_chunk = lambda a: jnp.pad(a.reshape(bd, tokens, RET_W),
                                      ((0, 0), (0, ret_pad - tokens), (0, 0))).reshape(bd * ret_pad, RET_W)
        ret, s_fin = _retention(pad_chunk(rq), pad_chunk(rk), pad_chunk(rv), state_ret[l], ret_pad, tokens)
        ret = ret.reshape(bd, ret_pad, RET_W)[:, :tokens].reshape(n_s, RET_W)
        y_s = _mix(y_s, gmix, sh_m, sc_m, gt_m, attn, ret, rg, gret, wga, wgr, wa, wr, wo, n_s)
        y_s = _moe(y_s, gffn, sh_f, sc_f, gt_f, w_route, b_route, w1, w3, w2, n_s)
        outs["ks"].append(k.reshape(bd, tokens, N_HEADS, HEAD_DIM))
        outs["vs"].append(v.reshape(bd, tokens, N_HEADS, HEAD_DIM))
        outs["kis"].append(ki.reshape(bd, tokens, IDX_DIM))
        outs["ss"].append(s_fin)

    st = lambda name: jnp.stack(outs[name])
    return (y_p.reshape(bp, seq, d), y_s.reshape(bd, tokens, d),
            st("kp"), st("vp"), st("kip"), st("sp"), st("ks"), st("vs"), st("kis"), st("ss"))
```

```python
import functools
import math

import numpy as np
import jax
import jax.numpy as jnp
from jax import lax
from jax.experimental import pallas as pl
from jax.experimental.pallas import tpu as pltpu

N_HEADS = 8
HEAD_DIM = 64
IDX_HEADS = 4
IDX_DIM = 64
TOPK_MAX = 256
PAGE_SIZE = 128
RET_HEADS = 4
RET_DK = 128
RET_DV = 128
RET_CHUNK = 128
N_GROUPS = 4
EXPERTS_PER_GROUP = 4
N_EXPERTS = N_GROUPS * EXPERTS_PER_GROUP
D_EXPERT = 256
ROPE_THETA = 10000.0
NORM_EPS = 1e-6
ATTN_W = N_HEADS * HEAD_DIM
RET_W = RET_HEADS * RET_DK

LANES = 128
VMEM_LIMIT = 56 * 1024 * 1024
NEG_BIG = -1e30
_Q_SCALE = HEAD_DIM ** -0.5 * math.log2(math.e)
INT_MIN = -2 ** 31

F32 = jnp.float32
BF16 = jnp.bfloat16
I32 = jnp.int32


def _dot(a, b):
    return jnp.dot(a, b, preferred_element_type=F32)


def _dot_nt(a, b):
    return lax.dot_general(a, b, (((1,), (1,)), ((), ())), preferred_element_type=F32)


def _dot_tn(a, b):
    return lax.dot_general(a, b, (((0,), (0,)), ((), ())), preferred_element_type=F32)


def _split3(a):
    a1 = a.astype(BF16)
    r1 = a - a1.astype(F32)
    a2 = r1.astype(BF16)
    a3 = (r1 - a2.astype(F32)).astype(BF16)
    return a1, a2, a3


def _dot3(a3, b3, dot=_dot, terms=6):
    a1, a2, a_3 = a3
    b1, b2, b_3 = b3
    mid = dot(a1, b2) + dot(a2, b1)
    if terms == 6:
        mid = mid + (dot(a1, b_3) + dot(a_3, b1) + dot(a2, b2))
    return dot(a1, b1) + mid


def _norm_mod(x, g, shift, scale):
    ms = jnp.mean(x * x, axis=-1, keepdims=True)
    y = x * lax.rsqrt(ms + NORM_EPS) * g
    return y * (1.0 + scale) + shift


def _lane_iota(shape):
    return lax.broadcasted_iota(I32, shape, len(shape) - 1)


def _rope(x, cos, sin, half):
    if 2 * half == LANES:
        swapped = pltpu.roll(x, half, 1)
    else:
        first = (_lane_iota(x.shape) & (2 * half - 1)) < half
        swapped = jnp.where(first, pltpu.roll(x, LANES - half, 1), pltpu.roll(x, half, 1))
    return x * cos + swapped * sin


def _group_ones(width):
    r = lax.broadcasted_iota(I32, (LANES, LANES), 0) // width
    c = lax.broadcasted_iota(I32, (LANES, LANES), 1) // width
    return jnp.where(r == c, 1.0, 0.0).astype(BF16)


def _group_mean_sq(x, ones, width):
    y1, y2, y3 = _split3(x * x)
    return (_dot(y1, ones) + (_dot(y2, ones) + _dot(y3, ones))) * (1.0 / width)


def _sortable(x):
    bits = lax.bitcast_convert_type(x, I32)
    return bits ^ ((bits >> 31) & jnp.int32(0x7FFFFFFF))


def _ada_kernel(c_ref, w_ref, b_ref, o_ref):
    c = c_ref[...]
    s = c * jax.nn.sigmoid(c)
    o_ref[...] = _dot3(_split3(s), _split3(w_ref[...])) + b_ref[...]


def _ada(c_all, w_ada, b_ada):
    m, d = c_all.shape
    n = w_ada.shape[1]
    tn = 1024
    return pl.pallas_call(
        _ada_kernel,
        grid=(n // tn,),
        in_specs=[pl.BlockSpec((m, d), lambda j: (0, 0)),
                  pl.BlockSpec((d, tn), lambda j: (0, j)),
                  pl.BlockSpec((1, tn), lambda j: (0, j))],
        out_specs=pl.BlockSpec((m, tn), lambda j: (0, j)),
        out_shape=jax.ShapeDtypeStruct((m, n), F32),
        compiler_params=pltpu.CompilerParams(vmem_limit_bytes=VMEM_LIMIT),
        name="ada",
    )(c_all, w_ada, b_ada.reshape(1, n))


_OFF_Q, _OFF_K, _OFF_V, _OFF_RQ, _OFF_RK, _OFF_RV, _OFF_RG = (0, 512, 1024, 1536, 2048, 2560, 3072)
_W_MAIN = 3584
_W_IDX = 512


def _split_concat(x, hi_half):
    x1, x2, _ = _split3(x)
    x1, x2 = x1.astype(F32), x2.astype(F32)
    lo = _lane_iota(x.shape) < LANES // 2
    r1, r2 = pltpu.roll(x1, LANES // 2, 1), pltpu.roll(x2, LANES // 2, 1)
    if hi_half:
        return jnp.where(lo, r1, x1).astype(BF16), jnp.where(lo, r2, 0.0).astype(BF16)
    return jnp.where(lo, x1, r1).astype(BF16), jnp.where(lo, x2, 0.0).astype(BF16)


def _proj_kernel(x_ref, g_ref, sh_ref, sc_ref, c64_ref, s64_ref, c128_ref, s128_ref,
                 wm_ref, wi_ref, gq_ref, gk_ref, gki_ref,
                 qb_ref, k_ref, kb_ref, v_ref, vb_ref, qi_ref, qcat_ref, ki_ref, kcat_ref, wo_ref,
                 rq_ref, rk_ref, rv_ref, rg_ref):
    h = _norm_mod(x_ref[...], g_ref[...], sh_ref[...], sc_ref[...])
    hb = h.astype(BF16)
    c64, s64 = c64_ref[...], s64_ref[...]
    c128, s128 = c128_ref[...], s128_ref[...]
    ones64 = _group_ones(HEAD_DIM)
    gq, gk = gq_ref[...], gk_ref[...]

    def seg(off, j):
        return _dot(hb, wm_ref[:, off + j * LANES: off + (j + 1) * LANES])

    for j in range(ATTN_W // LANES):
        sl = slice(j * LANES, (j + 1) * LANES)
        zq = seg(_OFF_Q, j)
        q = zq * lax.rsqrt(_group_mean_sq(zq, ones64, HEAD_DIM) + NORM_EPS) * gq
        qb_ref[:, sl] = (_rope(q, c64, s64, HEAD_DIM // 2) * _Q_SCALE).astype(BF16)
        zk = seg(_OFF_K, j)
        k = zk * lax.rsqrt(_group_mean_sq(zk, ones64, HEAD_DIM) + NORM_EPS) * gk
        k = _rope(k, c64, s64, HEAD_DIM // 2)
        k_ref[:, sl] = k
        kb_ref[:, sl] = k.astype(BF16)
        v = seg(_OFF_V, j)
        v_ref[:, sl] = v
        vb_ref[:, sl] = v.astype(BF16)
    for j in range(RET_W // LANES):
        sl = slice(j * LANES, (j + 1) * LANES)
        rq_ref[:, sl] = _rope(seg(_OFF_RQ, j), c128, s128, RET_DK // 2).astype(BF16)
        rk_ref[:, sl] = (_rope(seg(_OFF_RK, j), c128, s128, RET_DK // 2) * (RET_DK ** -0.5)).astype(BF16)
        rv_ref[:, sl] = seg(_OFF_RV, j).astype(BF16)
        rg_ref[:, sl] = seg(_OFF_RG, j)

    zi = _dot3(_split3(h), _split3(wi_ref[...]))
    for j in range(2):
        sl = slice(j * LANES, (j + 1) * LANES)
        qi = _rope(zi[:, sl], c64, s64, IDX_DIM // 2)
        qi_ref[:, sl] = qi
        for hi_half in (False, True):
            hd = 2 * j + hi_half
            a, b = _split_concat(qi * (IDX_DIM ** -0.5), hi_half)
            qcat_ref[:, 2 * hd * LANES:(2 * hd + 1) * LANES] = a
            qcat_ref[:, (2 * hd + 1) * LANES:(2 * hd + 2) * LANES] = b
    zk = zi[:, 2 * LANES: 3 * LANES]
    ms = jnp.sum(zk * zk, axis=-1, keepdims=True) * (1.0 / IDX_DIM)
    ki = _rope(zk * lax.rsqrt(ms + NORM_EPS) * gki_ref[...], c64, s64, IDX_DIM // 2)
    ki_ref[...] = ki[:, :IDX_DIM]
    k1, k2, _ = _split3(ki)
    k1, k2 = k1.astype(F32), k2.astype(F32)
    kcat_ref[:, :LANES] = (k1 + pltpu.roll(k2, LANES // 2, 1)).astype(BF16)
    kcat_ref[:, LANES:] = k1.astype(BF16)
    wo_ref[...] = zi[:, 3 * LANES: 4 * LANES]


def _proj(x, g, shift, scale, tabs, wm, wi, gq, gk, gki, tm):
    n, d = x.shape
    nt = n // tm
    groups, r, _ = shift.shape
    per = nt // groups if r == 1 else nt
    row = lambda w: pl.BlockSpec((tm, w), lambda i: (i, 0))
    full = lambda a: pl.BlockSpec(a.shape, lambda i: (0,) * a.ndim)
    mod = pl.BlockSpec((None, r, d), lambda i: (i // per, 0, 0))
    c64, s64, c128, s128 = tabs
    outs = [(ATTN_W, BF16), (ATTN_W, F32), (ATTN_W, BF16), (ATTN_W, F32), (ATTN_W, BF16),
            (IDX_HEADS * IDX_DIM, F32), (IDX_HEADS * 2 * LANES, BF16), (IDX_DIM, F32), (2 * LANES, BF16),
            (LANES, F32),
            (RET_W, BF16), (RET_W, BF16), (RET_W, BF16), (RET_W, F32)]
    return pl.pallas_call(
        _proj_kernel,
        grid=(nt,),
        in_specs=[row(d), full(g), mod, mod, row(LANES), row(LANES), row(LANES), row(LANES),
                  full(wm), full(wi), full(gq), full(gk), full(gki)],
        out_specs=[row(w) for w, _ in outs],
        out_shape=[jax.ShapeDtypeStruct((n, w), dt) for w, dt in outs],
        compiler_params=pltpu.CompilerParams(vmem_limit_bytes=VMEM_LIMIT),
        name="proj",
    )(x, g, shift, scale, c64, s64, c128, s128, wm, wi, gq, gk, gki)


def _select_bias(s_ref, b_ref, n_chunks, kc, topk, query_pos):
    nq = s_ref.shape[1]

    fold = 64

    def partial_sums(x):
        return jnp.sum(x.reshape(kc // fold, fold, nq), axis=0)

    def column_sum(x):
        return jnp.sum(partial_sums(x), axis=0, keepdims=True)

    def count(pred):
        def body(c, cnt):
            off = pl.multiple_of(c * kc, kc)
            return cnt + partial_sums(jnp.where(pred(s_ref[pl.ds(off, kc), :]), 1.0, 0.0))
        cnt = lax.fori_loop(0, n_chunks, body, jnp.zeros((fold, nq), F32))
        return jnp.sum(cnt, axis=0, keepdims=True)

    def value_bit(it, p):
        cand = p | lax.shift_left(jnp.int32(1), 31 - it)
        t = cand ^ jnp.int32(INT_MIN)
        c = count(lambda s: s >= t)
        return jnp.where(c >= topk, cand, p)

    p = lax.fori_loop(0, 32, value_bit, jnp.zeros((1, nq), I32))
    t_star = p ^ jnp.int32(INT_MIN)
    need = topk - count(lambda s: s > t_star)

    upto = jnp.where(lax.broadcasted_iota(I32, (kc, kc), 1) <= lax.broadcasted_iota(I32, (kc, kc), 0),
                     1.0, 0.0).astype(BF16)

    def write(c, ties_before):
        off = pl.multiple_of(c * kc, kc)
        s = s_ref[pl.ds(off, kc), :]
        key_pos = off + lax.broadcasted_iota(I32, (kc, nq), 0)
        tie = jnp.where(s == t_star, 1.0, 0.0)
        rank = ties_before + _dot(upto, tie.astype(BF16))
        tie_kept = jnp.where(rank <= need, tie, 0.0)
        bias = jnp.where(s > t_star, 0.0, jnp.where(tie_kept > 0.0, 0.0, NEG_BIG))
        b_ref[pl.ds(off, kc), :] = jnp.where(key_pos <= query_pos, bias, NEG_BIG)
        return ties_before + column_sum(tie)

    lax.fori_loop(0, n_chunks, write, jnp.zeros((1, nq), F32))


_SUB = 128


def _prompt_attend_kernel(qbt_ref, qcatt_ref, wt_ref, kcat_ref, kb_ref, vbt_ref, o_ref,
                          s_ref, b_ref, m_ref, l_ref, acc_ref, *, tq, kc, topk):
    i = pl.program_id(1)
    q0 = i * tq
    n_chunks = (q0 + tq + kc - 1) // kc
    query_pos = q0 + _lane_iota((1, tq))
    w = wt_ref[...] * (IDX_HEADS ** -0.5)

    def score_chunk(c, carry):
        for u in range(kc // _SUB):
            off = pl.multiple_of(c * kc + u * _SUB, _SUB)
            s = _dot(kcat_ref[pl.ds(off, _SUB), :], qcatt_ref[...])
            acc = None
            for hd in range(IDX_HEADS):
                term = jnp.maximum(s[:, hd * tq:(hd + 1) * tq], 0.0) * w[hd:hd + 1, :]
                acc = term if acc is None else acc + term
            key_pos = off + lax.broadcasted_iota(I32, (_SUB, tq), 0)
            s_ref[pl.ds(off, _SUB), :] = jnp.where(key_pos <= query_pos, _sortable(acc + 0.0),
                                                   jnp.int32(INT_MIN))
        return carry

    lax.fori_loop(0, n_chunks, score_chunk, 0)
    _select_bias(s_ref, b_ref, n_chunks, kc, topk, query_pos)

    m_ref[...] = jnp.full(m_ref.shape, NEG_BIG, F32)
    l_ref[...] = jnp.zeros(l_ref.shape, F32)
    acc_ref[...] = jnp.zeros(acc_ref.shape, F32)
    feat = lax.broadcasted_iota(I32, (LANES, tq), 0)

    def attend_chunk(c, carry):
        off = pl.multiple_of(c * kc, kc)
        for hd in range(N_HEADS):
            pair = slice(hd // 2 * LANES, (hd // 2 + 1) * LANES)
            qp = qbt_ref[pair, :]
            qh = jnp.where((feat < HEAD_DIM) == (hd % 2 == 0), qp, jnp.zeros_like(qp))
            m, l, acc = m_ref[hd], l_ref[hd], acc_ref[hd]
            for u in range(kc // _SUB):
                ks = pl.ds(pl.multiple_of(off + u * _SUB, _SUB), _SUB)
                s = _dot(kb_ref[ks, pair], qh) + b_ref[ks, :]
                m_new = jnp.maximum(m, jnp.max(s, axis=0, keepdims=True))
                a = jnp.exp2(m - m_new)
                p = jnp.exp2(s - m_new)
                l = a * l + jnp.sum(p, axis=0, keepdims=True)
                acc = a * acc + _dot(vbt_ref[hd * HEAD_DIM:(hd + 1) * HEAD_DIM, ks], p.astype(BF16))
                m = m_new
            m_ref[hd], l_ref[hd], acc_ref[hd] = m, l, acc
        return carry

    lax.fori_loop(0, n_chunks, attend_chunk, 0)
    for hd in range(N_HEADS):
        o_ref[hd * HEAD_DIM:(hd + 1) * HEAD_DIM, :] = acc_ref[hd] / l_ref[hd]


def _prompt_attend(qb, qcat, wi, kcat, kb, vb, batch, seq):
    tq, kc = 128, min(512, seq)
    topk = min(TOPK_MAX, seq // 4)
    nq = seq // tq
    n = qb.shape[0]
    qbt, vbt, wt = qb.T, vb.T, wi[:, :_ROWS].T
    qcatt = qcat.reshape(n // tq, tq, IDX_HEADS, 2 * LANES).transpose(3, 0, 2, 1).reshape(2 * LANES, n * IDX_HEADS)
    col = lambda rows: pl.BlockSpec((rows, tq), lambda b, i: (0, b * nq + i))
    res = lambda w: pl.BlockSpec((seq, w), lambda b, i: (b, 0), pipeline_mode=pl.Buffered(1))
    res_t = pl.BlockSpec((ATTN_W, seq), lambda b, i: (0, b), pipeline_mode=pl.Buffered(1))
    return pl.pallas_call(
        functools.partial(_prompt_attend_kernel, tq=tq, kc=kc, topk=topk),
        grid=(batch, nq),
        in_specs=[col(ATTN_W), pl.BlockSpec((qcatt.shape[0], IDX_HEADS * tq), lambda b, i: (0, b * nq + i)),
                  col(wt.shape[0]), res(kcat.shape[1]), res(ATTN_W), res_t],
        out_specs=col(ATTN_W),
        out_shape=jax.ShapeDtypeStruct((ATTN_W, batch * seq), F32),
        scratch_shapes=[pltpu.VMEM((seq, tq), I32), pltpu.VMEM((seq, tq), F32),
                        pltpu.VMEM((N_HEADS, 1, tq), F32), pltpu.VMEM((N_HEADS, 1, tq), F32),
                        pltpu.VMEM((N_HEADS, HEAD_DIM, tq), F32)],
        compiler_params=pltpu.CompilerParams(vmem_limit_bytes=VMEM_LIMIT,
                                             dimension_semantics=("arbitrary", "arbitrary")),
        name="prompt_attend",
    )(qbt, qcatt, wt, kcat, kb, vbt).T


_IDX_PAGES = 16
_KV_PAGES = 8
_ROWS = 8


def _sample_select_kernel(pt_ref, q_ref, w_ref, kn_ref, *rest, n_pages, topk, idx_bits, tokens):
    pages = rest[:_IDX_PAGES]
    bias_ref, s_ref = rest[_IDX_PAGES:]
    g = pl.program_id(1)
    q1, q2, _ = _split3(q_ref[...] * (IDX_DIM ** -0.5))
    q_cat = jnp.concatenate([q1, q1, q2, jnp.zeros_like(q1)], axis=1)
    w = w_ref[...] * (IDX_HEADS ** -0.5)

    def scores(keys_t):
        k1, k2, _ = _split3(keys_t)
        s = _dot(q_cat, jnp.concatenate([k1, k2, k1, jnp.zeros_like(k1)], axis=0))
        acc = None
        for hd in range(IDX_HEADS):
            term = jnp.maximum(s[hd * _ROWS:(hd + 1) * _ROWS], 0.0) * w[:, hd:hd + 1]
            acc = term if acc is None else acc + term
        return _sortable(acc + 0.0)

    sc = scores(jnp.concatenate([pages[u][...] for u in range(_IDX_PAGES)], axis=1))
    row0 = pl.multiple_of(g * _IDX_PAGES, _IDX_PAGES)
    for t in range(tokens):
        s_ref[t, pl.ds(row0, _IDX_PAGES), :] = jnp.concatenate(
            [sc[t:t + 1, u * PAGE_SIZE:(u + 1) * PAGE_SIZE] for u in range(_IDX_PAGES)], axis=0)

    @pl.when(g == pl.num_programs(1) - 1)
    def _():
        shape = (tokens, n_pages + _ROWS, LANES)
        lane1 = _lane_iota((1, LANES))
        sn = scores(kn_ref[...])
        floor = jnp.full((_ROWS - 1, LANES), INT_MIN, I32)
        for t in range(tokens):
            new_row = jnp.where(lane1 <= t, sn[t:t + 1], jnp.int32(INT_MIN))
            s_ref[t, n_pages:, :] = jnp.concatenate([new_row, floor], axis=0)

        idx = lax.broadcasted_iota(I32, shape, 1) * LANES + _lane_iota(shape)

        def count(hit):
            c = jnp.sum(jnp.where(hit, 1.0, 0.0), axis=1, keepdims=True)
            return jnp.sum(c, axis=2, keepdims=True)

        def value_bit(it, p):
            cand = p | lax.shift_left(jnp.int32(1), 31 - it)
            c = count(s_ref[...] >= (cand ^ jnp.int32(INT_MIN)))
            return jnp.where(c >= topk, cand, p)

        p = lax.fori_loop(0, 32, value_bit, jnp.zeros((tokens, 1, 1), I32))
        t_star = p ^ jnp.int32(INT_MIN)
        need = topk - count(s_ref[...] > t_star)

        def index_bit(it, r):
            cand = r | lax.shift_left(jnp.int32(1), idx_bits - 1 - it)
            c = count(jnp.where(idx < cand, s_ref[...], jnp.int32(INT_MIN)) == t_star)
            return jnp.where(c < need, cand, r)

        j_star = lax.fori_loop(0, idx_bits, index_bit, jnp.zeros((tokens, 1, 1), I32))
        s = s_ref[...]
        tie = jnp.where(idx <= j_star, s, jnp.int32(INT_MIN)) == t_star
        bias = jnp.where(s > t_star, 0.0, jnp.where(tie, 0.0, NEG_BIG))
        bias_ref[...] = jnp.where(s == jnp.int32(INT_MIN), NEG_BIG, bias)


def _sample_select(page_table, layer, kidx_t, q32, w8, ki_new_t, tokens):
    bd, n_pages = page_table.shape
    past = n_pages * PAGE_SIZE
    topk = min(TOPK_MAX, (past + tokens) // 4)
    idx_bits = (past + LANES - 1).bit_length()
    steps = n_pages // _IDX_PAGES
    page = lambda u: pl.BlockSpec((None, None, IDX_DIM, PAGE_SIZE),
                                  lambda b, g, pt: (layer, pt[b, g * _IDX_PAGES + u], 0, 0))
    per_b = lambda *blk: pl.BlockSpec((None,) + blk, lambda b, g, pt: (b,) + (0,) * len(blk))
    plane = (tokens, n_pages + _ROWS, LANES)
    grid_spec = pltpu.PrefetchScalarGridSpec(
        num_scalar_prefetch=1,
        grid=(bd, steps),
        in_specs=[per_b(IDX_HEADS * _ROWS, IDX_DIM), per_b(_ROWS, LANES), per_b(IDX_DIM, LANES)]
        + [page(u) for u in range(_IDX_PAGES)],
        out_specs=per_b(*plane),
        scratch_shapes=[pltpu.VMEM(plane, I32)],
    )
    return pl.pallas_call(
        functools.partial(_sample_select_kernel, n_pages=n_pages, topk=topk, idx_bits=idx_bits, tokens=tokens),
        grid_spec=grid_spec,
        out_shape=jax.ShapeDtypeStruct((bd,) + plane, F32),
        compiler_params=pltpu.CompilerParams(vmem_limit_bytes=VMEM_LIMIT,
                                             dimension_semantics=("arbitrary", "arbitrary")),
        name="sample_select",
    )(page_table, q32, w8, ki_new_t, *([kidx_t] * _IDX_PAGES))


def _sample_attend_kernel(pt_ref, q_ref, bias_ref, kn_ref, vn_ref, *rest, past, groups):
    kpages = rest[:_KV_PAGES]
    vpages = rest[_KV_PAGES:2 * _KV_PAGES]
    o_ref, s_ref, acc_ref = rest[2 * _KV_PAGES:]
    g = pl.program_id(1)
    width = _KV_PAGES * PAGE_SIZE

    @pl.when(g < groups)
    def _():
        keys_t = jnp.concatenate([kpages[u][...] for u in range(_KV_PAGES)], axis=1)
        s_ref[:, pl.ds(pl.multiple_of(g * width, width), width)] = _dot(q_ref[...], keys_t)

    @pl.when(g == groups - 1)
    def _():
        s_ref[:, past:] = _dot(q_ref[...], kn_ref[...])
        bias = bias_ref[...]
        for hd in range(N_HEADS):
            rows = slice(hd * _ROWS, (hd + 1) * _ROWS)
            s = s_ref[rows, :] + bias
            p = jnp.exp2(s - jnp.max(s, axis=-1, keepdims=True))
            s_ref[rows, :] = p / jnp.sum(p, axis=-1, keepdims=True)
        acc_ref[...] = jnp.zeros(acc_ref.shape, F32)

    @pl.when(g >= groups)
    def _():
        vals_t = jnp.concatenate([vpages[u][...] for u in range(_KV_PAGES)], axis=1)
        p = s_ref[:, pl.ds(pl.multiple_of((g - groups) * width, width), width)]
        acc_ref[...] += _dot_nt(p, vals_t)

    @pl.when(g == 2 * groups - 1)
    def _():
        acc = acc_ref[...] + _dot_nt(s_ref[:, past:], vn_ref[...])
        lane_head = _lane_iota((_ROWS, ATTN_W)) // HEAD_DIM
        out = jnp.zeros((_ROWS, ATTN_W), F32)
        for hd in range(N_HEADS):
            out = out + jnp.where(lane_head == hd, acc[hd * _ROWS:(hd + 1) * _ROWS], 0.0)
        o_ref[...] = out


def _sample_attend(page_table, layer, k_t, v_t, q_bd, bias, k_new_t, v_new_t):
    bd, n_pages = page_table.shape
    past = n_pages * PAGE_SIZE
    groups = n_pages // _KV_PAGES
    kpage = lambda u: pl.BlockSpec(
        (None, None, ATTN_W, PAGE_SIZE),
        lambda b, g, pt: (layer, pt[b, jnp.minimum(g, groups - 1) * _KV_PAGES + u], 0, 0))
    vpage = lambda u: pl.BlockSpec(
        (None, None, ATTN_W, PAGE_SIZE),
        lambda b, g, pt: (layer, pt[b, jnp.maximum(g - groups, 0) * _KV_PAGES + u], 0, 0))
    per_b = lambda r, w: pl.BlockSpec((None, r, w), lambda b, g, pt: (b, 0, 0))
    rows = N_HEADS * _ROWS
    grid_spec = pltpu.PrefetchScalarGridSpec(
        num_scalar_prefetch=1,
        grid=(bd, 2 * groups),
        in_specs=[per_b(rows, ATTN_W), per_b(_ROWS, past + LANES), per_b(ATTN_W, LANES), per_b(ATTN_W, LANES)]
        + [kpage(u) for u in range(_KV_PAGES)] + [vpage(u) for u in range(_KV_PAGES)],
        out_specs=per_b(_ROWS, ATTN_W),
        scratch_shapes=[pltpu.VMEM((rows, past + LANES), F32), pltpu.VMEM((rows, ATTN_W), F32)],
    )
    return pl.pallas_call(
        functools.partial(_sample_attend_kernel, past=past, groups=groups),
        grid_spec=grid_spec,
        out_shape=jax.ShapeDtypeStruct((bd, _ROWS, ATTN_W), F32),
        compiler_params=pltpu.CompilerParams(vmem_limit_bytes=VMEM_LIMIT,
                                             dimension_semantics=("arbitrary", "arbitrary")),
        name="sample_attend",
    )(page_table, q_bd, bias, k_new_t, v_new_t, *([k_t] * _KV_PAGES), *([v_t] * _KV_PAGES))


def _ret_kernel(q_ref, k_ref, v_ref, s0_ref, o_ref, sout_ref, state_ref, *, chunk, true_len):
    c = pl.program_id(1)

    @pl.when(c == 0)
    def _():
        state_ref[...] = s0_ref[...]

    ti = lax.broadcasted_iota(I32, (chunk, chunk), 0)
    tj = lax.broadcasted_iota(I32, (chunk, chunk), 1)
    diff = (ti - tj).astype(F32)
    t = lax.broadcasted_iota(I32, (chunk, 1), 0).astype(F32)
    for hd in range(RET_HEADS):
        sl = slice(hd * RET_DK, (hd + 1) * RET_DK)
        log_g = math.log(1.0 - 2.0 ** (-5.0 - hd))
        decay = jnp.where(diff >= 0, jnp.exp(log_g * jnp.maximum(diff, 0.0)), 0.0)
        q_dec = jnp.exp(log_g * (t + 1.0))
        k_dec = jnp.exp(log_g * (true_len - 1.0 - t))
        c_dec = math.exp(log_g * true_len)
        q, k, v = q_ref[:, sl], k_ref[:, sl], v_ref[:, sl]
        s = state_ref[hd]
        att = _dot_nt(q, k) * decay
        o_ref[:, sl] = _dot(att.astype(BF16), v) + _dot(q, s.astype(BF16)) * q_dec
        kd = (k.astype(F32) * k_dec).astype(BF16)
        state_ref[hd] = s * c_dec + _dot_tn(kd, v)

    @pl.when(c == pl.num_programs(1) - 1)
    def _():
        sout_ref[...] = state_ref[...]


def _retention(rq, rk, rv, state0, chunk, true_len):
    b = state0.shape[0]
    n = rq.shape[0] // (b * chunk)
    blk = pl.BlockSpec((chunk, RET_W), lambda i, c: (i * n + c, 0))
    st = pl.BlockSpec((None, RET_HEADS, RET_DK, RET_DV), lambda i, c: (i, 0, 0, 0))
    return pl.pallas_call(
        functools.partial(_ret_kernel, chunk=chunk, true_len=true_len),
        grid=(b, n),
        in_specs=[blk, blk, blk, st],
        out_specs=[blk, st],
        out_shape=[jax.ShapeDtypeStruct(rq.shape, F32), jax.ShapeDtypeStruct(state0.shape, F32)],
        scratch_shapes=[pltpu.VMEM((RET_HEADS, RET_DK, RET_DV), F32)],
        compiler_params=pltpu.CompilerParams(vmem_limit_bytes=VMEM_LIMIT,
                                             dimension_semantics=("arbitrary", "arbitrary")),
        name="retention",
    )(rq, rk, rv, state0)


def _mix_kernel(x_ref, g_ref, sh_ref, sc_ref, gt_ref, attn_ref, ret_ref, rg_ref, gret_ref,
                wga_ref, wgr_ref, wa_ref, wr_ref, wo_ref, o_ref):
    x = x_ref[...]
    hb = _norm_mod(x, g_ref[...], sh_ref[...], sc_ref[...]).astype(BF16)
    ret = ret_ref[...]
    rg = rg_ref[...]
    parts = []
    for hd in range(RET_HEADS):
        sl = slice(hd * RET_DV, (hd + 1) * RET_DV)
        r = ret[:, sl]
        ms = jnp.mean(r * r, axis=-1, keepdims=True)
        gate = rg[:, sl]
        parts.append(r * lax.rsqrt(ms + NORM_EPS) * gret_ref[:, sl] * (gate * jax.nn.sigmoid(gate)))
    retn = jnp.concatenate(parts, axis=-1).astype(BF16)
    y_a = _dot(attn_ref[...].astype(BF16), wa_ref[...])
    y_r = _dot(retn, wr_ref[...])
    ga = jax.nn.sigmoid(_dot(hb, wga_ref[...]))
    gr = jax.nn.sigmoid(_dot(hb, wgr_ref[...]))
    mix = _dot((ga * y_a + gr * y_r).astype(BF16), wo_ref[...])
    o_ref[...] = x + gt_ref[...] * mix


def _mix(x, g, shift, scale, gate, attn, ret, rg, gret, wga, wgr, wa, wr, wo, tm):
    n, d = x.shape
    nt = n // tm
    groups, r, _ = shift.shape
    per = nt // groups if r == 1 else nt
    row = lambda w: pl.BlockSpec((tm, w), lambda i: (i, 0))
    full = lambda a: pl.BlockSpec(a.shape, lambda i: (0,) * a.ndim)
    mod = pl.BlockSpec((None, r, d), lambda i: (i // per, 0, 0))
    return pl.pallas_call(
        _mix_kernel,
        grid=(nt,),
        in_specs=[row(d), full(g), mod, mod, mod, row(ATTN_W), row(RET_W), row(RET_W), full(gret),
                  full(wga), full(wgr), full(wa), full(wr), full(wo)],
        out_specs=row(d),
        out_shape=jax.ShapeDtypeStruct((n, d), F32),
        compiler_params=pltpu.CompilerParams(vmem_limit_bytes=VMEM_LIMIT),
        name="mix",
    )(x, g, shift, scale, gate, attn, ret, rg, gret, wga, wgr, wa, wr, wo)


_ROUTER_GROUP_LANE = N_EXPERTS


def _moe_kernel(x_ref, g_ref, sh_ref, sc_ref, gt_ref, wr_ref, br_ref, w1_ref, w3_ref, w2_ref,
                o_ref, hb_ref, comb_ref, acc_ref):
    e = pl.program_id(1)
    tm = x_ref.shape[0]
    lane = _lane_iota((tm, LANES))

    @pl.when(e == 0)
    def _():
        h = _norm_mod(x_ref[...], g_ref[...], sh_ref[...], sc_ref[...])
        hb_ref[...] = h.astype(BF16)
        logits = _dot3(_split3(h), _split3(wr_ref[...])) + br_ref[...]
        lane_f = lane.astype(F32)

        def first_lane(hit):
            return jnp.min(jnp.where(hit, lane_f, float(LANES)), axis=-1, keepdims=True)

        is_group = jnp.abs(lane_f - (_ROUTER_GROUP_LANE + (N_GROUPS - 1) / 2)) < N_GROUPS / 2
        gl = jnp.where(is_group, logits, -jnp.inf)
        gmax = jnp.max(gl, axis=-1, keepdims=True)
        g_idx = first_lane(gl == gmax) - _ROUTER_GROUP_LANE
        g_w = 1.0 / jnp.sum(jnp.exp(gl - gmax), axis=-1, keepdims=True)
        group_mid = g_idx * EXPERTS_PER_GROUP + (EXPERTS_PER_GROUP - 1) / 2
        el = jnp.where(jnp.abs(lane_f - group_mid) < EXPERTS_PER_GROUP / 2, logits, -jnp.inf)
        m1 = jnp.max(el, axis=-1, keepdims=True)
        i1 = first_lane(el == m1)
        el2 = jnp.where(lane_f == i1, -jnp.inf, el)
        m2 = jnp.max(el2, axis=-1, keepdims=True)
        i2 = first_lane(el2 == m2)
        e2 = jnp.exp(m2 - m1)
        v1 = 1.0 / (1.0 + e2)
        v2 = e2 / (1.0 + e2)
        comb_ref[...] = jnp.where(lane_f == i1, g_w * v1, jnp.where(lane_f == i2, g_w * v2, 0.0))
        acc_ref[...] = jnp.zeros(acc_ref.shape, F32)

    hb = hb_ref[...]
    a = _dot(hb, w1_ref[...])
    b = _dot(hb, w3_ref[...])
    ce = jnp.sum(jnp.where(lane == e, comb_ref[...], 0.0), axis=-1, keepdims=True)
    act = (a * jax.nn.sigmoid(a)) * b * ce
    acc_ref[...] += _dot(act.astype(BF16), w2_ref[...])

    @pl.when(e == pl.num_programs(1) - 1)
    def _():
        o_ref[...] = x_ref[...] + gt_ref[...] * acc_ref[...]


def _moe(x, g, shift, scale, gate, w_route, b_route, w1, w3, w2, tm):
    n, d = x.shape
    nt = n // tm
    groups, r, _ = shift.shape
    per = nt // groups if r == 1 else nt
    row = pl.BlockSpec((tm, d), lambda i, e: (i, 0))
    full = lambda a: pl.BlockSpec(a.shape, lambda i, e: (0,) * a.ndim)
    mod = pl.BlockSpec((None, r, d), lambda i, e: (i // per, 0, 0))
    return pl.pallas_call(
        _moe_kernel,
        grid=(nt, N_EXPERTS),
        in_specs=[row, full(g), mod, mod, mod, full(w_route), full(b_route),
                  pl.BlockSpec((None, d, D_EXPERT), lambda i, e: (e, 0, 0)),
                  pl.BlockSpec((None, d, D_EXPERT), lambda i, e: (e, 0, 0)),
                  pl.BlockSpec((None, D_EXPERT, d), lambda i, e: (e, 0, 0))],
        out_specs=row,
        out_shape=jax.ShapeDtypeStruct((n, d), F32),
        scratch_shapes=[pltpu.VMEM((tm, d), BF16), pltpu.VMEM((tm, LANES), F32), pltpu.VMEM((tm, d), F32)],
        compiler_params=pltpu.CompilerParams(vmem_limit_bytes=VMEM_LIMIT,
                                             dimension_semantics=("arbitrary", "arbitrary")),
        name="moe",
    )(x, g, shift, scale, gate, w_route, b_route, w1, w3, w2)


def _rope_tables(pos, dim):
    inv = ROPE_THETA ** (-jnp.arange(0, dim, 2, dtype=F32) / dim)
    ang = pos.astype(F32)[:, None] * inv[None, :]
    cos = jnp.concatenate([jnp.cos(ang), jnp.cos(ang)], axis=-1)
    sin = jnp.concatenate([-jnp.sin(ang), jnp.sin(ang)], axis=-1)
    rep = LANES // dim
    return jnp.tile(cos, (1, rep)), jnp.tile(sin, (1, rep))


def _pack_weights(l, w_in, g_q, g_k, g_kidx, w_group, b_group, w_router, b_router):
    offs = np.cumsum((ATTN_W, ATTN_W, ATTN_W, IDX_HEADS * IDX_DIM, IDX_DIM, IDX_HEADS,
                      RET_W, RET_W, RET_W, RET_W))
    w = w_in[l]
    d = w.shape[0]
    q, k, v, qi, ki, wi, rq, rk, rv, rg = (w[:, a:b] for a, b in zip((0, *offs[:-1]), offs))
    w_gate = w[:, offs[-1]:]
    wm = jnp.concatenate([q, k, v, rq, rk, rv, rg], axis=1).astype(BF16)
    zeros = lambda n: jnp.zeros((d, n), F32)
    widx = jnp.concatenate([qi, ki, zeros(LANES - IDX_DIM), wi, zeros(LANES - IDX_HEADS)], axis=1)
    half = w_gate.shape[1] // 2
    wga, wgr = w_gate[:, :half].astype(BF16), w_gate[:, half:].astype(BF16)
    gq = jnp.tile(g_q[l], LANES // HEAD_DIM)[None, :]
    gk = jnp.tile(g_k[l], LANES // HEAD_DIM)[None, :]
    gki = jnp.concatenate([g_kidx[l], jnp.zeros((LANES - IDX_DIM,), F32)])[None, :]
    pad = LANES - N_EXPERTS - N_GROUPS
    w_route = jnp.concatenate([w_router[l], w_group[l], zeros(pad)], axis=1)
    b_route = jnp.concatenate([b_router[l], b_group[l], jnp.zeros((pad,), F32)])[None, :]
    return wm, widx, wga, wgr, gq, gk, gki, w_route, b_route


def kernel(x_prompt, x_sample, cache_k, cache_v, cache_kidx, state_ret, page_table, c_prompt, c_sample,
           w_ada, b_ada, g_mix, g_ffn, w_in, g_q, g_k, g_kidx, g_ret, w_attn_out, w_ret_out, w_o,
           w_group, b_group, w_router, b_router, w_e1, w_e3, w_e2):
    bp, seq, d = x_prompt.shape
    bd, tokens, _ = x_sample.shape
    depth = w_in.shape[0]
    n_pages = page_table.shape[1]
    past = n_pages * PAGE_SIZE
    n_p, n_s = bp * seq, bd * tokens
    assert tokens <= _ROWS and n_pages % _IDX_PAGES == 0 and seq % RET_CHUNK == 0

    tabs_p = _rope_tables(jnp.arange(seq), HEAD_DIM) + _rope_tables(jnp.arange(seq), RET_DK)
    tabs_p = tuple(jnp.tile(t, (bp, 1)) for t in tabs_p)
    pos_s = past + jnp.arange(tokens)
    tabs_s = _rope_tables(pos_s, HEAD_DIM) + _rope_tables(pos_s, RET_DK)
    tabs_s = tuple(jnp.tile(t, (bd, 1)) for t in tabs_s)

    c_all = jnp.concatenate([c_prompt, c_sample], axis=0)
    c_rows = -(-c_all.shape[0] // 8) * 8
    c_all = jnp.pad(c_all, ((0, c_rows - c_all.shape[0]), (0, 0)))
    kidx_t = cache_kidx.transpose(0, 1, 3, 2)
    k_t = cache_k.transpose(0, 1, 3, 4, 2).reshape(cache_k.shape[:2] + (ATTN_W, PAGE_SIZE))
    v_t = cache_v.transpose(0, 1, 3, 4, 2).reshape(cache_v.shape[:2] + (ATTN_W, PAGE_SIZE))
    ret_pad = 16
    tm_p = 256 if seq % 256 == 0 else 128
    tm_moe = 1024 if seq % 1024 == 0 else tm_p

    y_p = x_prompt.reshape(n_p, d)
    y_s = x_sample.reshape(n_s, d)
    outs = {name: [] for name in ("kp", "vp", "kip", "sp", "ks", "vs", "kis", "ss")}
    for l in range(depth):
        wm, widx, wga, wgr, gq, gk, gki, w_route, b_route = _pack_weights(
            l, w_in, g_q, g_k, g_kidx, w_group, b_group, w_router, b_router)
        wa, wr, wo = w_attn_out[l].astype(BF16), w_ret_out[l].astype(BF16), w_o[l].astype(BF16)
        w1, w3, w2 = w_e1[l].astype(BF16), w_e3[l].astype(BF16), w_e2[l].astype(BF16)
        gmix, gffn, gret = g_mix[l][None, :], g_ffn[l][None, :], g_ret[l][None, :]

        ada = _ada(c_all, w_ada[l], b_ada[l])
        mods_p = [m[:bp, None, :] for m in jnp.split(ada, 6, axis=-1)]
        mods_s = [jnp.repeat(m[bp:bp + bd], tokens, axis=0)[None] for m in jnp.split(ada, 6, axis=-1)]

        sh_m, sc_m, gt_m, sh_f, sc_f, gt_f = mods_p
        (qb, k, kb, v, vb, qi, qcat, ki, kcat, wi, rq, rk, rv, rg) = _proj(
            y_p, gmix, sh_m, sc_m, tabs_p, wm, widx, gq, gk, gki, tm_p)
        attn = _prompt_attend(qb, qcat, wi, kcat, kb, vb, bp, seq)
        ret, s_fin = _retention(rq, rk, rv, jnp.zeros((bp, RET_HEADS, RET_DK, RET_DV), F32),
                                RET_CHUNK, RET_CHUNK)
        y_p = _mix(y_p, gmix, sh_m, sc_m, gt_m, attn, ret, rg, gret, wga, wgr, wa, wr, wo, tm_p)
        y_p = _moe(y_p, gffn, sh_f, sc_f, gt_f, w_route, b_route, w1, w3, w2, tm_moe)
        outs["kp"].append(k.reshape(bp, seq, N_HEADS, HEAD_DIM))
        outs["vp"].append(v.reshape(bp, seq, N_HEADS, HEAD_DIM))
        outs["kip"].append(ki.reshape(bp, seq, IDX_DIM))
        outs["sp"].append(s_fin)

        sh_m, sc_m, gt_m, sh_f, sc_f, gt_f = mods_s
        (qb, k, kb, v, vb, qi, qcat, ki, kcat, wi, rq, rk, rv, rg) = _proj(
            y_s, gmix, sh_m, sc_m, tabs_s, wm, widx, gq, gk, gki, n_s)
        pad_to = lambda a, rows: jnp.pad(a, ((0, 0), (0, rows - a.shape[1])) + ((0, 0),) * (a.ndim - 2))
        q32 = pad_to(qi.reshape(bd, tokens, IDX_HEADS, IDX_DIM), _ROWS).transpose(0, 2, 1, 3)
        q32 = q32.reshape(bd, IDX_HEADS * _ROWS, IDX_DIM)
        w8 = pad_to(wi.reshape(bd, tokens, LANES), _ROWS)
        new_t = lambda a: pad_to(a.reshape(bd, tokens, -1), LANES).transpose(0, 2, 1)
        plane = _sample_select(page_table, l, kidx_t, q32, w8, new_t(ki), tokens)
        bias = pad_to(plane[:, :, :n_pages + 1].reshape(bd, tokens, past + LANES), _ROWS)
        qh = pad_to(qb.astype(F32).reshape(bd, tokens, N_HEADS, HEAD_DIM), _ROWS).transpose(0, 2, 1, 3)
        q_bd = jnp.einsum('bhtd,hg->bhtgd', qh, jnp.eye(N_HEADS, dtype=F32)).reshape(bd, N_HEADS * _ROWS, ATTN_W)
        attn = _sample_attend(page_table, l, k_t, v_t, q_bd, bias, new_t(k), new_t(v))
        attn = attn[:, :tokens].reshape(n_s, ATTN_W)
        pad_chunk = lambda a: jnp.pad(a.reshape(bd, tokens, RET_W),
                                      ((0, 0), (0, ret_pad - tokens), (0, 0))).reshape(bd * ret_pad, RET_W)
        ret, s_fin = _retention(pad_chunk(rq), pad_chunk(rk), pad_chunk(rv), state_ret[l], ret_pad, tokens)
        ret = ret.reshape(bd, ret_pad, RET_W)[:, :tokens].reshape(n_s, RET_W)
        y_s = _mix(y_s, gmix, sh_m, sc_m, gt_m, attn, ret, rg, gret, wga, wgr, wa, wr, wo, n_s)
        y_s = _moe(y_s, gffn, sh_f, sc_f, gt_f, w_route, b_route, w1, w3, w2, n_s)
        outs["ks"].append(k.reshape(bd, tokens, N_HEADS, HEAD_DIM))
        outs["vs"].append(v.reshape(bd, tokens, N_HEADS, HEAD_DIM))
        outs["kis"].append(ki.reshape(bd, tokens, IDX_DIM))
        outs["ss"].append(s_fin)

    st = lambda name: jnp.stack(outs[name])
    return (y_p.reshape(bp, seq, d), y_s.reshape(bd, tokens, d),
            st("kp"), st("vp"), st("kip"), st("sp"), st("ks"), st("vs"), st("kis"), st("ss"))
```

```python
import functools
import math

import numpy as np
import jax
import jax.numpy as jnp
from jax import lax
from jax.experimental import pallas as pl
from jax.experimental.pallas import tpu as pltpu

N_HEADS = 8
HEAD_DIM = 64
IDX_HEADS = 4
IDX_DIM = 64
TOPK_MAX = 256
PAGE_SIZE = 128
RET_HEADS = 4
RET_DK = 128
RET_DV = 128
RET_CHUNK = 128
N_GROUPS = 4
EXPERTS_PER_GROUP = 4
N_EXPERTS = N_GROUPS * EXPERTS_PER_GROUP
D_EXPERT = 256
ROPE_THETA = 10000.0
NORM_EPS = 1e-6
ATTN_W = N_HEADS * HEAD_DIM
RET_W = RET_HEADS * RET_DK

LANES = 128
VMEM_LIMIT = 56 * 1024 * 1024
NEG_BIG = -1e30
_Q_SCALE = HEAD_DIM ** -0.5 * math.log2(math.e)
INT_MIN = -2 ** 31

F32 = jnp.float32
BF16 = jnp.bfloat16
I32 = jnp.int32


def _dot(a, b):
    return jnp.dot(a, b, preferred_element_type=F32)


def _dot_nt(a, b):
    return lax.dot_general(a, b, (((1,), (1,)), ((), ())), preferred_element_type=F32)


def _dot_tn(a, b):
    return lax.dot_general(a, b, (((0,), (0,)), ((), ())), preferred_element_type=F32)


def _split3(a):
    a1 = a.astype(BF16)
    r1 = a - a1.astype(F32)
    a2 = r1.astype(BF16)
    a3 = (r1 - a2.astype(F32)).astype(BF16)
    return a1, a2, a3


def _dot3(a3, b3, dot=_dot, terms=6):
    a1, a2, a_3 = a3
    b1, b2, b_3 = b3
    mid = dot(a1, b2) + dot(a2, b1)
    if terms == 6:
        mid = mid + (dot(a1, b_3) + dot(a_3, b1) + dot(a2, b2))
    return dot(a1, b1) + mid


def _norm_mod(x, g, shift, scale):
    ms = jnp.mean(x * x, axis=-1, keepdims=True)
    y = x * lax.rsqrt(ms + NORM_EPS) * g
    return y * (1.0 + scale) + shift


def _lane_iota(shape):
    return lax.broadcasted_iota(I32, shape, len(shape) - 1)


def _rope(x, cos, sin, half):
    if 2 * half == LANES:
        swapped = pltpu.roll(x, half, 1)
    else:
        first = (_lane_iota(x.shape) & (2 * half - 1)) < half
        swapped = jnp.where(first, pltpu.roll(x, LANES - half, 1), pltpu.roll(x, half, 1))
    return x * cos + swapped * sin


def _group_ones(width):
    r = lax.broadcasted_iota(I32, (LANES, LANES), 0) // width
    c = lax.broadcasted_iota(I32, (LANES, LANES), 1) // width
    return jnp.where(r == c, 1.0, 0.0).astype(BF16)


def _group_mean_sq(x, ones, width):
    y1, y2, y3 = _split3(x * x)
    return (_dot(y1, ones) + (_dot(y2, ones) + _dot(y3, ones))) * (1.0 / width)


def _sortable(x):
    bits = lax.bitcast_convert_type(x, I32)
    return bits ^ ((bits >> 31) & jnp.int32(0x7FFFFFFF))


def _ada_kernel(c_ref, w_ref, b_ref, o_ref):
    c = c_ref[...]
    s = c * jax.nn.sigmoid(c)
    o_ref[...] = _dot3(_split3(s), _split3(w_ref[...])) + b_ref[...]


def _ada(c_all, w_ada, b_ada):
    m, d = c_all.shape
    n = w_ada.shape[1]
    tn = 1024
    return pl.pallas_call(
        _ada_kernel,
        grid=(n // tn,),
        in_specs=[pl.BlockSpec((m, d), lambda j: (0, 0)),
                  pl.BlockSpec((d, tn), lambda j: (0, j)),
                  pl.BlockSpec((1, tn), lambda j: (0, j))],
        out_specs=pl.BlockSpec((m, tn), lambda j: (0, j)),
        out_shape=jax.ShapeDtypeStruct((m, n), F32),
        compiler_params=pltpu.CompilerParams(vmem_limit_bytes=VMEM_LIMIT),
        name="ada",
    )(c_all, w_ada, b_ada.reshape(1, n))


_OFF_Q, _OFF_K, _OFF_V, _OFF_RQ, _OFF_RK, _OFF_RV, _OFF_RG = (0, 512, 1024, 1536, 2048, 2560, 3072)
_W_MAIN = 3584
_W_IDX = 512


def _split_concat(x, hi_half):
    x1, x2, _ = _split3(x)
    x1, x2 = x1.astype(F32), x2.astype(F32)
    lo = _lane_iota(x.shape) < LANES // 2
    r1, r2 = pltpu.roll(x1, LANES // 2, 1), pltpu.roll(x2, LANES // 2, 1)
    if hi_half:
        return jnp.where(lo, r1, x1).astype(BF16), jnp.where(lo, r2, 0.0).astype(BF16)
    return jnp.where(lo, x1, r1).astype(BF16), jnp.where(lo, x2, 0.0).astype(BF16)


def _proj_kernel(x_ref, g_ref, sh_ref, sc_ref, c64_ref, s64_ref, c128_ref, s128_ref,
                 wm_ref, wi_ref, gq_ref, gk_ref, gki_ref,
                 qb_ref, k_ref, kb_ref, v_ref, vb_ref, qi_ref, qcat_ref, ki_ref, kcat_ref, wo_ref,
                 rq_ref, rk_ref, rv_ref, rg_ref):
    h = _norm_mod(x_ref[...], g_ref[...], sh_ref[...], sc_ref[...])
    hb = h.astype(BF16)
    c64, s64 = c64_ref[...], s64_ref[...]
    c128, s128 = c128_ref[...], s128_ref[...]
    ones64 = _group_ones(HEAD_DIM)
    gq, gk = gq_ref[...], gk_ref[...]

    def seg(off, j):
        return _dot(hb, wm_ref[:, off + j * LANES: off + (j + 1) * LANES])

    for j in range(ATTN_W // LANES):
        sl = slice(j * LANES, (j + 1) * LANES)
        zq = seg(_OFF_Q, j)
        q = zq * lax.rsqrt(_group_mean_sq(zq, ones64, HEAD_DIM) + NORM_EPS) * gq
        qb_ref[:, sl] = (_rope(q, c64, s64, HEAD_DIM // 2) * _Q_SCALE).astype(BF16)
        zk = seg(_OFF_K, j)
        k = zk * lax.rsqrt(_group_mean_sq(zk, ones64, HEAD_DIM) + NORM_EPS) * gk
        k = _rope(k, c64, s64, HEAD_DIM // 2)
        k_ref[:, sl] = k
        kb_ref[:, sl] = k.astype(BF16)
        v = seg(_OFF_V, j)
        v_ref[:, sl] = v
        vb_ref[:, sl] = v.astype(BF16)
    for j in range(RET_W // LANES):
        sl = slice(j * LANES, (j + 1) * LANES)
        rq_ref[:, sl] = _rope(seg(_OFF_RQ, j), c128, s128, RET_DK // 2).astype(BF16)
        rk_ref[:, sl] = (_rope(seg(_OFF_RK, j), c128, s128, RET_DK // 2) * (RET_DK ** -0.5)).astype(BF16)
        rv_ref[:, sl] = seg(_OFF_RV, j).astype(BF16)
        rg_ref[:, sl] = seg(_OFF_RG, j)

    zi = _dot3(_split3(h), _split3(wi_ref[...]), terms=3)
    for j in range(2):
        sl = slice(j * LANES, (j + 1) * LANES)
        qi = _rope(zi[:, sl], c64, s64, IDX_DIM // 2)
        qi_ref[:, sl] = qi
        for hi_half in (False, True):
            hd = 2 * j + hi_half
            a, b = _split_concat(qi * (IDX_DIM ** -0.5), hi_half)
            qcat_ref[:, 2 * hd * LANES:(2 * hd + 1) * LANES] = a
            qcat_ref[:, (2 * hd + 1) * LANES:(2 * hd + 2) * LANES] = b
    zk = zi[:, 2 * LANES: 3 * LANES]
    ms = jnp.sum(zk * zk, axis=-1, keepdims=True) * (1.0 / IDX_DIM)
    ki = _rope(zk * lax.rsqrt(ms + NORM_EPS) * gki_ref[...], c64, s64, IDX_DIM // 2)
    ki_ref[...] = ki[:, :IDX_DIM]
    k1, k2, _ = _split3(ki)
    k1, k2 = k1.astype(F32), k2.astype(F32)
    kcat_ref[:, :LANES] = (k1 + pltpu.roll(k2, LANES // 2, 1)).astype(BF16)
    kcat_ref[:, LANES:] = k1.astype(BF16)
    wo_ref[...] = zi[:, 3 * LANES: 4 * LANES]


def _proj(x, g, shift, scale, tabs, wm, wi, gq, gk, gki, tm):
    n, d = x.shape
    nt = n // tm
    groups, r, _ = shift.shape
    per = nt // groups if r == 1 else nt
    row = lambda w: pl.BlockSpec((tm, w), lambda i: (i, 0))
    full = lambda a: pl.BlockSpec(a.shape, lambda i: (0,) * a.ndim)
    mod = pl.BlockSpec((None, r, d), lambda i: (i // per, 0, 0))
    c64, s64, c128, s128 = tabs
    outs = [(ATTN_W, BF16), (ATTN_W, F32), (ATTN_W, BF16), (ATTN_W, F32), (ATTN_W, BF16),
            (IDX_HEADS * IDX_DIM, F32), (IDX_HEADS * 2 * LANES, BF16), (IDX_DIM, F32), (2 * LANES, BF16),
            (LANES, F32),
            (RET_W, BF16), (RET_W, BF16), (RET_W, BF16), (RET_W, F32)]
    return pl.pallas_call(
        _proj_kernel,
        grid=(nt,),
        in_specs=[row(d), full(g), mod, mod, row(LANES), row(LANES), row(LANES), row(LANES),
                  full(wm), full(wi), full(gq), full(gk), full(gki)],
        out_specs=[row(w) for w, _ in outs],
        out_shape=[jax.ShapeDtypeStruct((n, w), dt) for w, dt in outs],
        compiler_params=pltpu.CompilerParams(vmem_limit_bytes=VMEM_LIMIT),
        name="proj",
    )(x, g, shift, scale, c64, s64, c128, s128, wm, wi, gq, gk, gki)


def _select_bias(s_ref, h_ref, b_ref, n_chunks, kc, topk, query_pos):
    nq = s_ref.shape[1]
    i16 = jnp.int16
    half = 1 << 15
    fold = 64

    def partial_sums(x):
        parts = [x[j * fold:(j + 1) * fold] for j in range(kc // fold)]
        while len(parts) > 1:
            parts = [a + b for a, b in zip(parts[::2], parts[1::2])] + parts[len(parts) & ~1:]
        return parts[0]

    def column_sum(x):
        return jnp.sum(partial_sums(x), axis=0, keepdims=True)

    def count(pred):
        def body(c, cnt):
            off = pl.multiple_of(c * kc, kc)
            return cnt + partial_sums(jnp.where(pred(s_ref[pl.ds(off, kc), :]), 1.0, 0.0))
        cnt = lax.fori_loop(0, n_chunks, body, jnp.zeros((fold, nq), F32))
        return jnp.sum(cnt, axis=0, keepdims=True)

    def count16(pred):
        def body(c, cnt):
            off = pl.multiple_of(c * kc, kc)
            hit = jnp.where(pred(h_ref[pl.ds(off, kc), :]), i16(1), i16(0))
            return cnt + partial_sums(hit)
        cnt = lax.fori_loop(0, n_chunks, body, jnp.zeros((fold, nq), i16))
        return jnp.sum(cnt.astype(F32), axis=0, keepdims=True)

    def high_bit(it, p):
        cand = p | lax.shift_left(jnp.int32(1), 15 - it)
        t = (cand - half).astype(i16)
        return jnp.where(count16(lambda h: h >= t) >= topk, cand, p)

    hi_star = lax.fori_loop(0, 16, high_bit, jnp.zeros((1, nq), I32)) - half
    hi16 = hi_star.astype(i16)
    above = count16(lambda h: h > hi16)

    def low_plane(c, carry):
        off = pl.multiple_of(c * kc, kc)
        s = s_ref[pl.ds(off, kc), :]
        low = (s & 0xFFFF) - half
        h_ref[pl.ds(off, kc), :] = jnp.where((s >> 16) == hi_star, low, -half).astype(i16)
        return carry

    lax.fori_loop(0, n_chunks, low_plane, 0)

    def low_bit(it, p):
        cand = p | lax.shift_left(jnp.int32(1), 15 - it)
        t = (cand - half).astype(i16)
        return jnp.where(above + count16(lambda h: h >= t) >= topk, cand, p)

    lo_star = lax.fori_loop(0, 16, low_bit, jnp.zeros((1, nq), I32))
    t_star = hi_star * (1 << 16) + lo_star
    need = topk - count(lambda s: s > t_star)

    upto = jnp.where(lax.broadcasted_iota(I32, (kc, kc), 1) <= lax.broadcasted_iota(I32, (kc, kc), 0),
                     1.0, 0.0).astype(BF16)

    def write(c, ties_before):
        off = pl.multiple_of(c * kc, kc)
        tie = jnp.where(s_ref[pl.ds(off, kc), :] == t_star, 1.0, 0.0)
        tie_b = tie.astype(BF16)
        for u in range(kc // _SUB):
            sub = pl.ds(pl.multiple_of(off + u * _SUB, _SUB), _SUB)
            s = s_ref[sub, :]
            key_pos = off + u * _SUB + lax.broadcasted_iota(I32, (_SUB, nq), 0)
            rank = ties_before + _dot(upto[u * _SUB:(u + 1) * _SUB], tie_b)
            tie_kept = jnp.where(rank <= need, s, t_star + 1) == t_star
            bias = jnp.where(s > t_star, 0.0, jnp.where(tie_kept, 0.0, NEG_BIG))
            b_ref[sub, :] = jnp.where(key_pos <= query_pos, bias, NEG_BIG)
        return ties_before + column_sum(tie)

    lax.fori_loop(0, n_chunks, write, jnp.zeros((1, nq), F32))


_SUB = 128


def _prompt_attend_kernel(qbt_ref, qcatt_ref, wt_ref, kcat_ref, kb_ref, vbt_ref, o_ref,
                          s_ref, h_ref, b_ref, m_ref, l_ref, acc_ref, *, tq, kc, topk):
    i = pl.program_id(1)
    q0 = i * tq
    n_chunks = (q0 + tq + kc - 1) // kc
    query_pos = q0 + _lane_iota((1, tq))
    w = wt_ref[...] * (IDX_HEADS ** -0.5)

    def score_chunk(c, carry):
        for u in range(kc // _SUB):
            off = pl.multiple_of(c * kc + u * _SUB, _SUB)
            s = _dot(kcat_ref[pl.ds(off, _SUB), :], qcatt_ref[...])
            acc = None
            for hd in range(IDX_HEADS):
                term = jnp.maximum(s[:, hd * tq:(hd + 1) * tq], 0.0) * w[hd:hd + 1, :]
                acc = term if acc is None else acc + term
            key_pos = off + lax.broadcasted_iota(I32, (_SUB, tq), 0)
            key = jnp.where(key_pos <= query_pos, _sortable(acc + 0.0), jnp.int32(INT_MIN))
            s_ref[pl.ds(off, _SUB), :] = key
            h_ref[pl.ds(off, _SUB), :] = (key >> 16).astype(jnp.int16)
        return carry

    lax.fori_loop(0, n_chunks, score_chunk, 0)
    _select_bias(s_ref, h_ref, b_ref, n_chunks, kc, topk, query_pos)

    m_ref[...] = jnp.full(m_ref.shape, NEG_BIG, F32)
    l_ref[...] = jnp.zeros(l_ref.shape, F32)
    acc_ref[...] = jnp.zeros(acc_ref.shape, F32)
    feat = lax.broadcasted_iota(I32, (LANES, tq), 0)

    def attend_chunk(c, carry):
        off = pl.multiple_of(c * kc, kc)
        for hd in range(N_HEADS):
            pair = slice(hd // 2 * LANES, (hd // 2 + 1) * LANES)
            qp = qbt_ref[pair, :]
            qh = jnp.where((feat < HEAD_DIM) == (hd % 2 == 0), qp, jnp.zeros_like(qp))
            m, l, acc = m_ref[hd], l_ref[hd], acc_ref[hd]
            for u in range(kc // _SUB):
                ks = pl.ds(pl.multiple_of(off + u * _SUB, _SUB), _SUB)
                s = _dot(kb_ref[ks, pair], qh) + b_ref[ks, :]
                m_new = jnp.maximum(m, jnp.max(s, axis=0, keepdims=True))
                a = jnp.exp2(m - m_new)
                p = jnp.exp2(s - m_new)
                l = a * l + jnp.sum(p, axis=0, keepdims=True)
                acc = a * acc + _dot(vbt_ref[hd * HEAD_DIM:(hd + 1) * HEAD_DIM, ks], p.astype(BF16))
                m = m_new
            m_ref[hd], l_ref[hd], acc_ref[hd] = m, l, acc
        return carry

    lax.fori_loop(0, n_chunks, attend_chunk, 0)
    for hd in range(N_HEADS):
        o_ref[hd * HEAD_DIM:(hd + 1) * HEAD_DIM, :] = acc_ref[hd] / l_ref[hd]


def _prompt_attend(qb, qcat, wi, kcat, kb, vb, batch, seq):
    tq, kc = 128, min(512, seq)
    topk = min(TOPK_MAX, seq // 4)
    nq = seq // tq
    n = qb.shape[0]
    qbt, vbt, wt = qb.T, vb.T, wi[:, :_ROWS].T
    qcatt = qcat.reshape(n // tq, tq, IDX_HEADS, 2 * LANES).transpose(3, 0, 2, 1).reshape(2 * LANES, n * IDX_HEADS)
    col = lambda rows: pl.BlockSpec((rows, tq), lambda b, i: (0, b * nq + i))
    res = lambda w: pl.BlockSpec((seq, w), lambda b, i: (b, 0), pipeline_mode=pl.Buffered(1))
    res_t = pl.BlockSpec((ATTN_W, seq), lambda b, i: (0, b), pipeline_mode=pl.Buffered(1))
    return pl.pallas_call(
        functools.partial(_prompt_attend_kernel, tq=tq, kc=kc, topk=topk),
        grid=(batch, nq),
        in_specs=[col(ATTN_W), pl.BlockSpec((qcatt.shape[0], IDX_HEADS * tq), lambda b, i: (0, b * nq + i)),
                  col(wt.shape[0]), res(kcat.shape[1]), res(ATTN_W), res_t],
        out_specs=col(ATTN_W),
        out_shape=jax.ShapeDtypeStruct((ATTN_W, batch * seq), F32),
        scratch_shapes=[pltpu.VMEM((seq, tq), I32), pltpu.VMEM((seq, tq), jnp.int16), pltpu.VMEM((seq, tq), F32),
                        pltpu.VMEM((N_HEADS, 1, tq), F32), pltpu.VMEM((N_HEADS, 1, tq), F32),
                        pltpu.VMEM((N_HEADS, HEAD_DIM, tq), F32)],
        compiler_params=pltpu.CompilerParams(vmem_limit_bytes=VMEM_LIMIT,
                                             dimension_semantics=("arbitrary", "arbitrary")),
        name="prompt_attend",
    )(qbt, qcatt, wt, kcat, kb, vbt).T


_IDX_PAGES = 16
_KV_PAGES = 16
_ROWS = 8


def _sample_select_kernel(pt_ref, q_ref, w_ref, kn_ref, *rest, n_pages, topk, idx_bits, tokens):
    pages = rest[:_IDX_PAGES]
    bias_ref, s_ref = rest[_IDX_PAGES:]
    g = pl.program_id(1)
    q1, q2, _ = _split3(q_ref[...] * (IDX_DIM ** -0.5))
    q_cat = jnp.concatenate([q1, q1, q2, jnp.zeros_like(q1)], axis=1)
    w = w_ref[...] * (IDX_HEADS ** -0.5)

    def scores(keys_t):
        k1, k2, _ = _split3(keys_t)
        s = _dot(q_cat, jnp.concatenate([k1, k2, k1, jnp.zeros_like(k1)], axis=0))
        acc = None
        for hd in range(IDX_HEADS):
            term = jnp.maximum(s[hd * _ROWS:(hd + 1) * _ROWS], 0.0) * w[:, hd:hd + 1]
            acc = term if acc is None else acc + term
        return _sortable(acc + 0.0)

    sc = scores(jnp.concatenate([pages[u][...] for u in range(_IDX_PAGES)], axis=1))
    row0 = pl.multiple_of(g * _IDX_PAGES, _IDX_PAGES)
    for t in range(tokens):
        s_ref[t, pl.ds(row0, _IDX_PAGES), :] = jnp.concatenate(
            [sc[t:t + 1, u * PAGE_SIZE:(u + 1) * PAGE_SIZE] for u in range(_IDX_PAGES)], axis=0)

    @pl.when(g == pl.num_programs(1) - 1)
    def _():
        shape = (tokens, n_pages + _ROWS, LANES)
        lane1 = _lane_iota((1, LANES))
        sn = scores(kn_ref[...])
        floor = jnp.full((_ROWS - 1, LANES), INT_MIN, I32)
        for t in range(tokens):
            new_row = jnp.where(lane1 <= t, sn[t:t + 1], jnp.int32(INT_MIN))
            s_ref[t, n_pages:, :] = jnp.concatenate([new_row, floor], axis=0)

        idx = lax.broadcasted_iota(I32, shape, 1) * LANES + _lane_iota(shape)

        def count(hit):
            c = jnp.sum(jnp.where(hit, 1.0, 0.0), axis=1, keepdims=True)
            return jnp.sum(c, axis=2, keepdims=True)

        def value_bit(it, p):
            cand = p | lax.shift_left(jnp.int32(1), 31 - it)
            c = count(s_ref[...] >= (cand ^ jnp.int32(INT_MIN)))
            return jnp.where(c >= topk, cand, p)

        p = lax.fori_loop(0, 32, value_bit, jnp.zeros((tokens, 1, 1), I32))
        t_star = p ^ jnp.int32(INT_MIN)
        need = topk - count(s_ref[...] > t_star)

        def index_bit(it, r):
            cand = r | lax.shift_left(jnp.int32(1), idx_bits - 1 - it)
            c = count(jnp.where(idx < cand, s_ref[...], jnp.int32(INT_MIN)) == t_star)
            return jnp.where(c < need, cand, r)

        j_star = lax.fori_loop(0, idx_bits, index_bit, jnp.zeros((tokens, 1, 1), I32))
        s = s_ref[...]
        tie = jnp.where(idx <= j_star, s, jnp.int32(INT_MIN)) == t_star
        bias = jnp.where(s > t_star, 0.0, jnp.where(tie, 0.0, NEG_BIG))
        bias_ref[...] = jnp.where(s == jnp.int32(INT_MIN), NEG_BIG, bias)


def _sample_select(page_table, layer, kidx_t, q32, w8, ki_new_t, tokens):
    bd, n_pages = page_table.shape
    past = n_pages * PAGE_SIZE
    topk = min(TOPK_MAX, (past + tokens) // 4)
    idx_bits = (past + LANES - 1).bit_length()
    steps = n_pages // _IDX_PAGES
    page = lambda u: pl.BlockSpec((None, None, IDX_DIM, PAGE_SIZE),
                                  lambda b, g, pt: (layer, pt[b, g * _IDX_PAGES + u], 0, 0))
    per_b = lambda *blk: pl.BlockSpec((None,) + blk, lambda b, g, pt: (b,) + (0,) * len(blk))
    plane = (tokens, n_pages + _ROWS, LANES)
    grid_spec = pltpu.PrefetchScalarGridSpec(
        num_scalar_prefetch=1,
        grid=(bd, steps),
        in_specs=[per_b(IDX_HEADS * _ROWS, IDX_DIM), per_b(_ROWS, LANES), per_b(IDX_DIM, LANES)]
        + [page(u) for u in range(_IDX_PAGES)],
        out_specs=per_b(*plane),
        scratch_shapes=[pltpu.VMEM(plane, I32)],
    )
    return pl.pallas_call(
        functools.partial(_sample_select_kernel, n_pages=n_pages, topk=topk, idx_bits=idx_bits, tokens=tokens),
        grid_spec=grid_spec,
        out_shape=jax.ShapeDtypeStruct((bd,) + plane, F32),
        compiler_params=pltpu.CompilerParams(vmem_limit_bytes=VMEM_LIMIT,
                                             dimension_semantics=("arbitrary", "arbitrary")),
        name="sample_select",
    )(page_table, q32, w8, ki_new_t, *([kidx_t] * _IDX_PAGES))


def _sample_attend_kernel(pt_ref, q_ref, bias_ref, kn_ref, vn_ref, *rest, past, groups):
    kpages = rest[:_KV_PAGES]
    vpages = rest[_KV_PAGES:2 * _KV_PAGES]
    o_ref, s_ref, acc_ref = rest[2 * _KV_PAGES:]
    g = pl.program_id(1)
    width = _KV_PAGES * PAGE_SIZE

    @pl.when(g < groups)
    def _():
        keys_t = jnp.concatenate([kpages[u][...] for u in range(_KV_PAGES)], axis=1)
        s_ref[:, pl.ds(pl.multiple_of(g * width, width), width)] = _dot(q_ref[...], keys_t)

    @pl.when(g == groups - 1)
    def _():
        s_ref[:, past:] = _dot(q_ref[...], kn_ref[...])
        bias = bias_ref[...]
        for hd in range(N_HEADS):
            rows = slice(hd * _ROWS, (hd + 1) * _ROWS)
            s = s_ref[rows, :] + bias
            p = jnp.exp2(s - jnp.max(s, axis=-1, keepdims=True))
            s_ref[rows, :] = p / jnp.sum(p, axis=-1, keepdims=True)
        acc_ref[...] = jnp.zeros(acc_ref.shape, F32)

    @pl.when(g >= groups)
    def _():
        vals_t = jnp.concatenate([vpages[u][...] for u in range(_KV_PAGES)], axis=1)
        p = s_ref[:, pl.ds(pl.multiple_of((g - groups) * width, width), width)]
        acc_ref[...] += _dot_nt(p, vals_t)

    @pl.when(g == 2 * groups - 1)
    def _():
        acc = acc_ref[...] + _dot_nt(s_ref[:, past:], vn_ref[...])
        lane_head = _lane_iota((_ROWS, ATTN_W)) // HEAD_DIM
        out = jnp.zeros((_ROWS, ATTN_W), F32)
        for hd in range(N_HEADS):
            out = out + jnp.where(lane_head == hd, acc[hd * _ROWS:(hd + 1) * _ROWS], 0.0)
        o_ref[...] = out


def _sample_attend(page_table, layer, k_t, v_t, q_bd, bias, k_new_t, v_new_t):
    bd, n_pages = page_table.shape
    past = n_pages * PAGE_SIZE
    groups = n_pages // _KV_PAGES
    kpage = lambda u: pl.BlockSpec(
        (None, None, ATTN_W, PAGE_SIZE),
        lambda b, g, pt: (layer, pt[b, jnp.minimum(g, groups - 1) * _KV_PAGES + u], 0, 0))
    vpage = lambda u: pl.BlockSpec(
        (None, None, ATTN_W, PAGE_SIZE),
        lambda b, g, pt: (layer, pt[b, jnp.maximum(g - groups, 0) * _KV_PAGES + u], 0, 0))
    per_b = lambda r, w: pl.BlockSpec((None, r, w), lambda b, g, pt: (b, 0, 0))
    rows = N_HEADS * _ROWS
    grid_spec = pltpu.PrefetchScalarGridSpec(
        num_scalar_prefetch=1,
        grid=(bd, 2 * groups),
        in_specs=[per_b(rows, ATTN_W), per_b(_ROWS, past + LANES), per_b(ATTN_W, LANES), per_b(ATTN_W, LANES)]
        + [kpage(u) for u in range(_KV_PAGES)] + [vpage(u) for u in range(_KV_PAGES)],
        out_specs=per_b(_ROWS, ATTN_W),
        scratch_shapes=[pltpu.VMEM((rows, past + LANES), F32), pltpu.VMEM((rows, ATTN_W), F32)],
    )
    return pl.pallas_call(
        functools.partial(_sample_attend_kernel, past=past, groups=groups),
        grid_spec=grid_spec,
        out_shape=jax.ShapeDtypeStruct((bd, _ROWS, ATTN_W), F32),
        compiler_params=pltpu.CompilerParams(vmem_limit_bytes=VMEM_LIMIT,
                                             dimension_semantics=("arbitrary", "arbitrary")),
        name="sample_attend",
    )(page_table, q_bd, bias, k_new_t, v_new_t, *([k_t] * _KV_PAGES), *([v_t] * _KV_PAGES))


def _ret_kernel(q_ref, k_ref, v_ref, s0_ref, o_ref, sout_ref, state_ref, *, chunk, true_len):
    c = pl.program_id(1)

    @pl.when(c == 0)
    def _():
        state_ref[...] = s0_ref[...]

    ti = lax.broadcasted_iota(I32, (chunk, chunk), 0)
    tj = lax.broadcasted_iota(I32, (chunk, chunk), 1)
    diff = (ti - tj).astype(F32)
    t = lax.broadcasted_iota(I32, (chunk, 1), 0).astype(F32)
    for hd in range(RET_HEADS):
        sl = slice(hd * RET_DK, (hd + 1) * RET_DK)
        log_g = math.log(1.0 - 2.0 ** (-5.0 - hd))
        decay = jnp.where(diff >= 0, jnp.exp(log_g * jnp.maximum(diff, 0.0)), 0.0)
        q_dec = jnp.exp(log_g * (t + 1.0))
        k_dec = jnp.exp(log_g * (true_len - 1.0 - t))
        c_dec = math.exp(log_g * true_len)
        q, k, v = q_ref[:, sl], k_ref[:, sl], v_ref[:, sl]
        s = state_ref[hd]
        att = _dot_nt(q, k) * decay
        o_ref[:, sl] = _dot(att.astype(BF16), v) + _dot(q, s.astype(BF16)) * q_dec
        kd = (k.astype(F32) * k_dec).astype(BF16)
        state_ref[hd] = s * c_dec + _dot_tn(kd, v)

    @pl.when(c == pl.num_programs(1) - 1)
    def _():
        sout_ref[...] = state_ref[...]


def _retention(rq, rk, rv, state0, chunk, true_len):
    b = state0.shape[0]
    n = rq.shape[0] // (b * chunk)
    blk = pl.BlockSpec((chunk, RET_W), lambda i, c: (i * n + c, 0))
    st = pl.BlockSpec((None, RET_HEADS, RET_DK, RET_DV), lambda i, c: (i, 0, 0, 0))
    return pl.pallas_call(
        functools.partial(_ret_kernel, chunk=chunk, true_len=true_len),
        grid=(b, n),
        in_specs=[blk, blk, blk, st],
        out_specs=[blk, st],
        out_shape=[jax.ShapeDtypeStruct(rq.shape, F32), jax.ShapeDtypeStruct(state0.shape, F32)],
        scratch_shapes=[pltpu.VMEM((RET_HEADS, RET_DK, RET_DV), F32)],
        compiler_params=pltpu.CompilerParams(vmem_limit_bytes=VMEM_LIMIT,
                                             dimension_semantics=("arbitrary", "arbitrary")),
        name="retention",
    )(rq, rk, rv, state0)


def _mix_kernel(x_ref, g_ref, sh_ref, sc_ref, gt_ref, attn_ref, ret_ref, rg_ref, gret_ref,
                wga_ref, wgr_ref, wa_ref, wr_ref, wo_ref, o_ref):
    x = x_ref[...]
    hb = _norm_mod(x, g_ref[...], sh_ref[...], sc_ref[...]).astype(BF16)
    ret = ret_ref[...]
    rg = rg_ref[...]
    parts = []
    for hd in range(RET_HEADS):
        sl = slice(hd * RET_DV, (hd + 1) * RET_DV)
        r = ret[:, sl]
        ms = jnp.mean(r * r, axis=-1, keepdims=True)
        gate = rg[:, sl]
        parts.append(r * lax.rsqrt(ms + NORM_EPS) * gret_ref[:, sl] * (gate * jax.nn.sigmoid(gate)))
    retn = jnp.concatenate(parts, axis=-1).astype(BF16)
    y_a = _dot(attn_ref[...].astype(BF16), wa_ref[...])
    y_r = _dot(retn, wr_ref[...])
    ga = jax.nn.sigmoid(_dot(hb, wga_ref[...]))
    gr = jax.nn.sigmoid(_dot(hb, wgr_ref[...]))
    mix = _dot((ga * y_a + gr * y_r).astype(BF16), wo_ref[...])
    o_ref[...] = x + gt_ref[...] * mix


def _mix(x, g, shift, scale, gate, attn, ret, rg, gret, wga, wgr, wa, wr, wo, tm):
    n, d = x.shape
    nt = n // tm
    groups, r, _ = shift.shape
    per = nt // groups if r == 1 else nt
    row = lambda w: pl.BlockSpec((tm, w), lambda i: (i, 0))
    full = lambda a: pl.BlockSpec(a.shape, lambda i: (0,) * a.ndim)
    mod = pl.BlockSpec((None, r, d), lambda i: (i // per, 0, 0))
    return pl.pallas_call(
        _mix_kernel,
        grid=(nt,),
        in_specs=[row(d), full(g), mod, mod, mod, row(ATTN_W), row(RET_W), row(RET_W), full(gret),
                  full(wga), full(wgr), full(wa), full(wr), full(wo)],
        out_specs=row(d),
        out_shape=jax.ShapeDtypeStruct((n, d), F32),
        compiler_params=pltpu.CompilerParams(vmem_limit_bytes=VMEM_LIMIT),
        name="mix",
    )(x, g, shift, scale, gate, attn, ret, rg, gret, wga, wgr, wa, wr, wo)


_ROUTER_GROUP_LANE = N_EXPERTS


def _moe_kernel(x_ref, g_ref, sh_ref, sc_ref, gt_ref, wr_ref, br_ref, w1_ref, w3_ref, w2_ref,
                o_ref, hb_ref, comb_ref, acc_ref):
    e = pl.program_id(1)
    tm = x_ref.shape[0]
    lane = _lane_iota((tm, LANES))

    @pl.when(e == 0)
    def _():
        h = _norm_mod(x_ref[...], g_ref[...], sh_ref[...], sc_ref[...])
        hb_ref[...] = h.astype(BF16)
        logits = _dot3(_split3(h), _split3(wr_ref[...])) + br_ref[...]
        lane_f = lane.astype(F32)

        def first_lane(hit):
            return jnp.min(jnp.where(hit, lane_f, float(LANES)), axis=-1, keepdims=True)

        is_group = jnp.abs(lane_f - (_ROUTER_GROUP_LANE + (N_GROUPS - 1) / 2)) < N_GROUPS / 2
        gl = jnp.where(is_group, logits, -jnp.inf)
        gmax = jnp.max(gl, axis=-1, keepdims=True)
        g_idx = first_lane(gl == gmax) - _ROUTER_GROUP_LANE
        g_w = 1.0 / jnp.sum(jnp.exp(gl - gmax), axis=-1, keepdims=True)
        group_mid = g_idx * EXPERTS_PER_GROUP + (EXPERTS_PER_GROUP - 1) / 2
        el = jnp.where(jnp.abs(lane_f - group_mid) < EXPERTS_PER_GROUP / 2, logits, -jnp.inf)
        m1 = jnp.max(el, axis=-1, keepdims=True)
        i1 = first_lane(el == m1)
        el2 = jnp.where(lane_f == i1, -jnp.inf, el)
        m2 = jnp.max(el2, axis=-1, keepdims=True)
        i2 = first_lane(el2 == m2)
        e2 = jnp.exp(m2 - m1)
        v1 = 1.0 / (1.0 + e2)
        v2 = e2 / (1.0 + e2)
        comb_ref[...] = jnp.where(lane_f == i1, g_w * v1, jnp.where(lane_f == i2, g_w * v2, 0.0))
        acc_ref[...] = jnp.zeros(acc_ref.shape, F32)

    hb = hb_ref[...]
    a = _dot(hb, w1_ref[...])
    b = _dot(hb, w3_ref[...])
    ce = jnp.sum(jnp.where(lane == e, comb_ref[...], 0.0), axis=-1, keepdims=True)
    act = (a * jax.nn.sigmoid(a)) * b * ce
    acc_ref[...] += _dot(act.astype(BF16), w2_ref[...])

    @pl.when(e == pl.num_programs(1) - 1)
    def _():
        o_ref[...] = x_ref[...] + gt_ref[...] * acc_ref[...]


def _moe(x, g, shift, scale, gate, w_route, b_route, w1, w3, w2, tm):
    n, d = x.shape
    nt = n // tm
    groups, r, _ = shift.shape
    per = nt // groups if r == 1 else nt
    row = pl.BlockSpec((tm, d), lambda i, e: (i, 0))
    full = lambda a: pl.BlockSpec(a.shape, lambda i, e: (0,) * a.ndim)
    mod = pl.BlockSpec((None, r, d), lambda i, e: (i // per, 0, 0))
    return pl.pallas_call(
        _moe_kernel,
        grid=(nt, N_EXPERTS),
        in_specs=[row, full(g), mod, mod, mod, full(w_route), full(b_route),
                  pl.BlockSpec((None, d, D_EXPERT), lambda i, e: (e, 0, 0)),
                  pl.BlockSpec((None, d, D_EXPERT), lambda i, e: (e, 0, 0)),
                  pl.BlockSpec((None, D_EXPERT, d), lambda i, e: (e, 0, 0))],
        out_specs=row,
        out_shape=jax.ShapeDtypeStruct((n, d), F32),
        scratch_shapes=[pltpu.VMEM((tm, d), BF16), pltpu.VMEM((tm, LANES), F32), pltpu.VMEM((tm, d), F32)],
        compiler_params=pltpu.CompilerParams(vmem_limit_bytes=VMEM_LIMIT,
                                             dimension_semantics=("arbitrary", "arbitrary")),
        name="moe",
    )(x, g, shift, scale, gate, w_route, b_route, w1, w3, w2)


def _rope_tables(pos, dim):
    inv = ROPE_THETA ** (-jnp.arange(0, dim, 2, dtype=F32) / dim)
    ang = pos.astype(F32)[:, None] * inv[None, :]
    cos = jnp.concatenate([jnp.cos(ang), jnp.cos(ang)], axis=-1)
    sin = jnp.concatenate([-jnp.sin(ang), jnp.sin(ang)], axis=-1)
    rep = LANES // dim
    return jnp.tile(cos, (1, rep)), jnp.tile(sin, (1, rep))


def _pack_weights(l, w_in, g_q, g_k, g_kidx, w_group, b_group, w_router, b_router):
    offs = np.cumsum((ATTN_W, ATTN_W, ATTN_W, IDX_HEADS * IDX_DIM, IDX_DIM, IDX_HEADS,
                      RET_W, RET_W, RET_W, RET_W))
    w = w_in[l]
    d = w.shape[0]
    q, k, v, qi, ki, wi, rq, rk, rv, rg = (w[:, a:b] for a, b in zip((0, *offs[:-1]), offs))
    w_gate = w[:, offs[-1]:]
    wm = jnp.concatenate([q, k, v, rq, rk, rv, rg], axis=1).astype(BF16)
    zeros = lambda n: jnp.zeros((d, n), F32)
    widx = jnp.concatenate([qi, ki, zeros(LANES - IDX_DIM), wi, zeros(LANES - IDX_HEADS)], axis=1)
    half = w_gate.shape[1] // 2
    wga, wgr = w_gate[:, :half].astype(BF16), w_gate[:, half:].astype(BF16)
    gq = jnp.tile(g_q[l], LANES // HEAD_DIM)[None, :]
    gk = jnp.tile(g_k[l], LANES // HEAD_DIM)[None, :]
    gki = jnp.concatenate([g_kidx[l], jnp.zeros((LANES - IDX_DIM,), F32)])[None, :]
    pad = LANES - N_EXPERTS - N_GROUPS
    w_route = jnp.concatenate([w_router[l], w_group[l], zeros(pad)], axis=1)
    b_route = jnp.concatenate([b_router[l], b_group[l], jnp.zeros((pad,), F32)])[None, :]
    return wm, widx, wga, wgr, gq, gk, gki, w_route, b_route


def kernel(x_prompt, x_sample, cache_k, cache_v, cache_kidx, state_ret, page_table, c_prompt, c_sample,
           w_ada, b_ada, g_mix, g_ffn, w_in, g_q, g_k, g_kidx, g_ret, w_attn_out, w_ret_out, w_o,
           w_group, b_group, w_router, b_router, w_e1, w_e3, w_e2):
    bp, seq, d = x_prompt.shape
    bd, tokens, _ = x_sample.shape
    depth = w_in.shape[0]
    n_pages = page_table.shape[1]
    past = n_pages * PAGE_SIZE
    n_p, n_s = bp * seq, bd * tokens
    assert tokens <= _ROWS and n_pages % _IDX_PAGES == 0 and seq % RET_CHUNK == 0

    tabs_p = _rope_tables(jnp.arange(seq), HEAD_DIM) + _rope_tables(jnp.arange(seq), RET_DK)
    tabs_p = tuple(jnp.tile(t, (bp, 1)) for t in tabs_p)
    pos_s = past + jnp.arange(tokens)
    tabs_s = _rope_tables(pos_s, HEAD_DIM) + _rope_tables(pos_s, RET_DK)
    tabs_s = tuple(jnp.tile(t, (bd, 1)) for t in tabs_s)

    c_all = jnp.concatenate([c_prompt, c_sample], axis=0)
    c_rows = -(-c_all.shape[0] // 8) * 8
    c_all = jnp.pad(c_all, ((0, c_rows - c_all.shape[0]), (0, 0)))
    kidx_t = cache_kidx.transpose(0, 1, 3, 2)
    k_t = cache_k.transpose(0, 1, 3, 4, 2).reshape(cache_k.shape[:2] + (ATTN_W, PAGE_SIZE))
    v_t = cache_v.transpose(0, 1, 3, 4, 2).reshape(cache_v.shape[:2] + (ATTN_W, PAGE_SIZE))
    ret_pad = 16
    tm_p = 256 if seq % 256 == 0 else 128
    tm_moe = 1024 if seq % 1024 == 0 else tm_p

    y_p = x_prompt.reshape(n_p, d)
    y_s = x_sample.reshape(n_s, d)
    outs = {name: [] for name in ("kp", "vp", "kip", "sp", "ks", "vs", "kis", "ss")}
    for l in range(depth):
        wm, widx, wga, wgr, gq, gk, gki, w_route, b_route = _pack_weights(
            l, w_in, g_q, g_k, g_kidx, w_group, b_group, w_router, b_router)
        wa, wr, wo = w_attn_out[l].astype(BF16), w_ret_out[l].astype(BF16), w_o[l].astype(BF16)
        w1, w3, w2 = w_e1[l].astype(BF16), w_e3[l].astype(BF16), w_e2[l].astype(BF16)
        gmix, gffn, gret = g_mix[l][None, :], g_ffn[l][None, :], g_ret[l][None, :]

        ada = _ada(c_all, w_ada[l], b_ada[l])
        mods_p = [m[:bp, None, :] for m in jnp.split(ada, 6, axis=-1)]
        mods_s = [jnp.repeat(m[bp:bp + bd], tokens, axis=0)[None] for m in jnp.split(ada, 6, axis=-1)]

        sh_m, sc_m, gt_m, sh_f, sc_f, gt_f = mods_p
        (qb, k, kb, v, vb, qi, qcat, ki, kcat, wi, rq, rk, rv, rg) = _proj(
            y_p, gmix, sh_m, sc_m, tabs_p, wm, widx, gq, gk, gki, tm_p)
        attn = _prompt_attend(qb, qcat, wi, kcat, kb, vb, bp, seq)
        ret, s_fin = _retention(rq, rk, rv, jnp.zeros((bp, RET_HEADS, RET_DK, RET_DV), F32),
                                RET_CHUNK, RET_CHUNK)
        y_p = _mix(y_p, gmix, sh_m, sc_m, gt_m, attn, ret, rg, gret, wga, wgr, wa, wr, wo, tm_p)
        y_p = _moe(y_p, gffn, sh_f, sc_f, gt_f, w_route, b_route, w1, w3, w2, tm_moe)
        outs["kp"].append(k.reshape(bp, seq, N_HEADS, HEAD_DIM))
        outs["vp"].append(v.reshape(bp, seq, N_HEADS, HEAD_DIM))
        outs["kip"].append(ki.reshape(bp, seq, IDX_DIM))
        outs["sp"].append(s_fin)

        sh_m, sc_m, gt_m, sh_f, sc_f, gt_f = mods_s
        (qb, k, kb, v, vb, qi, qcat, ki, kcat, wi, rq, rk, rv, rg) = _proj(
            y_s, gmix, sh_m, sc_m, tabs_s, wm, widx, gq, gk, gki, n_s)
        pad_to = lambda a, rows: jnp.pad(a, ((0, 0), (0, rows - a.shape[1])) + ((0, 0),) * (a.ndim - 2))
        q32 = pad_to(qi.reshape(bd, tokens, IDX_HEADS, IDX_DIM), _ROWS).transpose(0, 2, 1, 3)
        q32 = q32.reshape(bd, IDX_HEADS * _ROWS, IDX_DIM)
        w8 = pad_to(wi.reshape(bd, tokens, LANES), _ROWS)
        new_t = lambda a: pad_to(a.reshape(bd, tokens, -1), LANES).transpose(0, 2, 1)
        plane = _sample_select(page_table, l, kidx_t, q32, w8, new_t(ki), tokens)
        bias = pad_to(plane[:, :, :n_pages + 1].reshape(bd, tokens, past + LANES), _ROWS)
        qh = pad_to(qb.astype(F32).reshape(bd, tokens, N_HEADS, HEAD_DIM), _ROWS).transpose(0, 2, 1, 3)
        q_bd = jnp.einsum('bhtd,hg->bhtgd', qh, jnp.eye(N_HEADS, dtype=F32)).reshape(bd, N_HEADS * _ROWS, ATTN_W)
        attn = _sample_attend(page_table, l, k_t, v_t, q_bd, bias, new_t(k), new_t(v))
        attn = attn[:, :tokens].reshape(n_s, ATTN_W)
        pad_chunk = lambda a: jnp.pad(a.reshape(bd, tokens, RET_W),
                                      ((0, 0), (0, ret_pad - tokens), (0, 0))).reshape(bd * ret_pad, RET_W)
        ret, s_fin = _retention(pad_chunk(rq), pad_chunk(rk), pad_chunk(rv), state_ret[l], ret_pad, tokens)
        ret = ret.reshape(bd, ret_pad, RET_W)[:, :tokens].reshape(n_s, RET_W)
        y_s = _mix(y_s, gmix, sh_m, sc_m, gt_m, attn, ret, rg, gret, wga, wgr, wa, wr, wo, n_s)
        y_s = _moe(y_s, gffn, sh_f, sc_f, gt_f, w_route, b_route, w1, w3, w2, n_s)
        outs["ks"].append(k.reshape(bd, tokens, N_HEADS, HEAD_DIM))
        outs["vs"].append(v.reshape(bd, tokens, N_HEADS, HEAD_DIM))
        outs["kis"].append(ki.reshape(bd, tokens, IDX_DIM))
        outs["ss"].append(s_fin)

    st = lambda name: jnp.stack(outs[name])
    return (y_p.reshape(bp, seq, d), y_s.reshape(bd, tokens, d),
            st("kp"), st("vp"), st("kip"), st("sp"), st("ks"), st("vs"), st("kis"), st("ss"))
```

```python
import functools
import math

import numpy as np
import jax
import jax.numpy as jnp
from jax import lax
from jax.experimental import pallas as pl
from jax.experimental.pallas import tpu as pltpu

N_HEADS = 8
HEAD_DIM = 64
IDX_HEADS = 4
IDX_DIM = 64
TOPK_MAX = 256
PAGE_SIZE = 128
RET_HEADS = 4
RET_DK = 128
RET_DV = 128
RET_CHUNK = 128
N_GROUPS = 4
EXPERTS_PER_GROUP = 4
N_EXPERTS = N_GROUPS * EXPERTS_PER_GROUP
D_EXPERT = 256
ROPE_THETA = 10000.0
NORM_EPS = 1e-6
ATTN_W = N_HEADS * HEAD_DIM
RET_W = RET_HEADS * RET_DK

LANES = 128
VMEM_LIMIT = 56 * 1024 * 1024
NEG_BIG = -1e30
_Q_SCALE = HEAD_DIM ** -0.5 * math.log2(math.e)
INT_MIN = -2 ** 31

F32 = jnp.float32
BF16 = jnp.bfloat16
I32 = jnp.int32


def _dot(a, b):
    return jnp.dot(a, b, preferred_element_type=F32)


def _dot_nt(a, b):
    return lax.dot_general(a, b, (((1,), (1,)), ((), ())), preferred_element_type=F32)


def _dot_tn(a, b):
    return lax.dot_general(a, b, (((0,), (0,)), ((), ())), preferred_element_type=F32)


def _split3(a):
    a1 = a.astype(BF16)
    r1 = a - a1.astype(F32)
    a2 = r1.astype(BF16)
    a3 = (r1 - a2.astype(F32)).astype(BF16)
    return a1, a2, a3


def _dot3(a3, b3, dot=_dot, terms=6):
    a1, a2, a_3 = a3
    b1, b2, b_3 = b3
    mid = dot(a1, b2) + dot(a2, b1)
    if terms == 6:
        mid = mid + (dot(a1, b_3) + dot(a_3, b1) + dot(a2, b2))
    return dot(a1, b1) + mid


def _norm_mod(x, g, shift, scale):
    ms = jnp.mean(x * x, axis=-1, keepdims=True)
    y = x * lax.rsqrt(ms + NORM_EPS) * g
    return y * (1.0 + scale) + shift


def _lane_iota(shape):
    return lax.broadcasted_iota(I32, shape, len(shape) - 1)


def _rope(x, cos, sin, half):
    if 2 * half == LANES:
        swapped = pltpu.roll(x, half, 1)
    else:
        first = (_lane_iota(x.shape) & (2 * half - 1)) < half
        swapped = jnp.where(first, pltpu.roll(x, LANES - half, 1), pltpu.roll(x, half, 1))
    return x * cos + swapped * sin


def _group_ones(width):
    r = lax.broadcasted_iota(I32, (LANES, LANES), 0) // width
    c = lax.broadcasted_iota(I32, (LANES, LANES), 1) // width
    return jnp.where(r == c, 1.0, 0.0).astype(BF16)


def _group_mean_sq(x, ones, width):
    y1, y2, y3 = _split3(x * x)
    return (_dot(y1, ones) + (_dot(y2, ones) + _dot(y3, ones))) * (1.0 / width)


def _sortable(x):
    bits = lax.bitcast_convert_type(x, I32)
    return bits ^ ((bits >> 31) & jnp.int32(0x7FFFFFFF))


def _ada_kernel(c_ref, w_ref, b_ref, o_ref):
    c = c_ref[...]
    s = c * jax.nn.sigmoid(c)
    o_ref[...] = _dot3(_split3(s), _split3(w_ref[...])) + b_ref[...]


def _ada(c_all, w_ada, b_ada):
    m, d = c_all.shape
    n = w_ada.shape[1]
    tn = 1024
    return pl.pallas_call(
        _ada_kernel,
        grid=(n // tn,),
        in_specs=[pl.BlockSpec((m, d), lambda j: (0, 0)),
                  pl.BlockSpec((d, tn), lambda j: (0, j)),
                  pl.BlockSpec((1, tn), lambda j: (0, j))],
        out_specs=pl.BlockSpec((m, tn), lambda j: (0, j)),
        out_shape=jax.ShapeDtypeStruct((m, n), F32),
        compiler_params=pltpu.CompilerParams(vmem_limit_bytes=VMEM_LIMIT),
        name="ada",
    )(c_all, w_ada, b_ada.reshape(1, n))


_OFF_Q, _OFF_K, _OFF_V, _OFF_RQ, _OFF_RK, _OFF_RV, _OFF_RG = (0, 512, 1024, 1536, 2048, 2560, 3072)
_W_MAIN = 3584
_W_IDX = 512


def _split_concat(x, hi_half):
    x1, x2, _ = _split3(x)
    x1, x2 = x1.astype(F32), x2.astype(F32)
    lo = _lane_iota(x.shape) < LANES // 2
    r1, r2 = pltpu.roll(x1, LANES // 2, 1), pltpu.roll(x2, LANES // 2, 1)
    if hi_half:
        return jnp.where(lo, r1, x1).astype(BF16), jnp.where(lo, r2, 0.0).astype(BF16)
    return jnp.where(lo, x1, r1).astype(BF16), jnp.where(lo, x2, 0.0).astype(BF16)


def _proj_kernel(x_ref, g_ref, sh_ref, sc_ref, c64_ref, s64_ref, c128_ref, s128_ref,
                 wm_ref, wi_ref, gq_ref, gk_ref, gki_ref,
                 qb_ref, k_ref, kb_ref, v_ref, vb_ref, qi_ref, qcat_ref, ki_ref, kcat_ref, wo_ref,
                 rq_ref, rk_ref, rv_ref, rg_ref):
    h = _norm_mod(x_ref[...], g_ref[...], sh_ref[...], sc_ref[...])
    hb = h.astype(BF16)
    c64, s64 = c64_ref[...], s64_ref[...]
    c128, s128 = c128_ref[...], s128_ref[...]
    ones64 = _group_ones(HEAD_DIM)
    gq, gk = gq_ref[...], gk_ref[...]

    def seg(off, j):
        return _dot(hb, wm_ref[:, off + j * LANES: off + (j + 1) * LANES])

    for j in range(ATTN_W // LANES):
        sl = slice(j * LANES, (j + 1) * LANES)
        zq = seg(_OFF_Q, j)
        q = zq * lax.rsqrt(_group_mean_sq(zq, ones64, HEAD_DIM) + NORM_EPS) * gq
        qb_ref[:, sl] = (_rope(q, c64, s64, HEAD_DIM // 2) * _Q_SCALE).astype(BF16)
        zk = seg(_OFF_K, j)
        k = zk * lax.rsqrt(_group_mean_sq(zk, ones64, HEAD_DIM) + NORM_EPS) * gk
        k = _rope(k, c64, s64, HEAD_DIM // 2)
        k_ref[:, sl] = k
        kb_ref[:, sl] = k.astype(BF16)
        v = seg(_OFF_V, j)
        v_ref[:, sl] = v
        vb_ref[:, sl] = v.astype(BF16)
    for j in range(RET_W // LANES):
        sl = slice(j * LANES, (j + 1) * LANES)
        rq_ref[:, sl] = _rope(seg(_OFF_RQ, j), c128, s128, RET_DK // 2).astype(BF16)
        rk_ref[:, sl] = (_rope(seg(_OFF_RK, j), c128, s128, RET_DK // 2) * (RET_DK ** -0.5)).astype(BF16)
        rv_ref[:, sl] = seg(_OFF_RV, j).astype(BF16)
        rg_ref[:, sl] = seg(_OFF_RG, j)

    zi = _dot3(_split3(h), _split3(wi_ref[...]), terms=3)
    for j in range(2):
        sl = slice(j * LANES, (j + 1) * LANES)
        qi = _rope(zi[:, sl], c64, s64, IDX_DIM // 2)
        qi_ref[:, sl] = qi
        for hi_half in (False, True):
            hd = 2 * j + hi_half
            a, b = _split_concat(qi * (IDX_DIM ** -0.5), hi_half)
            qcat_ref[:, 2 * hd * LANES:(2 * hd + 1) * LANES] = a
            qcat_ref[:, (2 * hd + 1) * LANES:(2 * hd + 2) * LANES] = b
    zk = zi[:, 2 * LANES: 3 * LANES]
    ms = jnp.sum(zk * zk, axis=-1, keepdims=True) * (1.0 / IDX_DIM)
    ki = _rope(zk * lax.rsqrt(ms + NORM_EPS) * gki_ref[...], c64, s64, IDX_DIM // 2)
    ki_ref[...] = ki[:, :IDX_DIM]
    k1, k2, _ = _split3(ki)
    k1, k2 = k1.astype(F32), k2.astype(F32)
    kcat_ref[:, :LANES] = (k1 + pltpu.roll(k2, LANES // 2, 1)).astype(BF16)
    kcat_ref[:, LANES:] = k1.astype(BF16)
    wo_ref[...] = zi[:, 3 * LANES: 4 * LANES]


def _proj(x, g, shift, scale, tabs, wm, wi, gq, gk, gki, tm):
    n, d = x.shape
    nt = n // tm
    groups, r, _ = shift.shape
    per = nt // groups if r == 1 else nt
    row = lambda w: pl.BlockSpec((tm, w), lambda i: (i, 0))
    full = lambda a: pl.BlockSpec(a.shape, lambda i: (0,) * a.ndim)
    mod = pl.BlockSpec((None, r, d), lambda i: (i // per, 0, 0))
    c64, s64, c128, s128 = tabs
    outs = [(ATTN_W, BF16), (ATTN_W, F32), (ATTN_W, BF16), (ATTN_W, F32), (ATTN_W, BF16),
            (IDX_HEADS * IDX_DIM, F32), (IDX_HEADS * 2 * LANES, BF16), (IDX_DIM, F32), (2 * LANES, BF16),
            (LANES, F32),
            (RET_W, BF16), (RET_W, BF16), (RET_W, BF16), (RET_W, F32)]
    return pl.pallas_call(
        _proj_kernel,
        grid=(nt,),
        in_specs=[row(d), full(g), mod, mod, row(LANES), row(LANES), row(LANES), row(LANES),
                  full(wm), full(wi), full(gq), full(gk), full(gki)],
        out_specs=[row(w) for w, _ in outs],
        out_shape=[jax.ShapeDtypeStruct((n, w), dt) for w, dt in outs],
        compiler_params=pltpu.CompilerParams(vmem_limit_bytes=VMEM_LIMIT),
        name="proj",
    )(x, g, shift, scale, c64, s64, c128, s128, wm, wi, gq, gk, gki)


def _select_bias(s_ref, b_ref, n_chunks, kc, topk, query_pos):
    nq = s_ref.shape[1]
    fold = 64

    def partial_sums(x):
        parts = [x[j * fold:(j + 1) * fold] for j in range(kc // fold)]
        while len(parts) > 1:
            parts = [a + b for a, b in zip(parts[::2], parts[1::2])] + parts[len(parts) & ~1:]
        return parts[0]

    def column_sum(x):
        return jnp.sum(partial_sums(x), axis=0, keepdims=True)

    def count(pred):
        def body(c, cnt):
            off = pl.multiple_of(c * kc, kc)
            return cnt + partial_sums(jnp.where(pred(s_ref[pl.ds(off, kc), :]), 1.0, 0.0))
        cnt = lax.fori_loop(0, n_chunks, body, jnp.zeros((fold, nq), F32))
        return jnp.sum(cnt, axis=0, keepdims=True)

    def value_bit(it, p):
        cand = p | lax.shift_left(jnp.int32(1), 31 - it)
        t = cand ^ jnp.int32(INT_MIN)
        c = count(lambda s: s >= t)
        return jnp.where(c >= topk, cand, p)

    p = lax.fori_loop(0, 32, value_bit, jnp.zeros((1, nq), I32))
    t_star = p ^ jnp.int32(INT_MIN)
    need = topk - count(lambda s: s > t_star)

    upto = jnp.where(lax.broadcasted_iota(I32, (kc, kc), 1) <= lax.broadcasted_iota(I32, (kc, kc), 0),
                     1.0, 0.0).astype(BF16)

    def write(c, ties_before):
        off = pl.multiple_of(c * kc, kc)
        tie = jnp.where(s_ref[pl.ds(off, kc), :] == t_star, 1.0, 0.0)
        tie_b = tie.astype(BF16)
        for u in range(kc // _SUB):
            sub = pl.ds(pl.multiple_of(off + u * _SUB, _SUB), _SUB)
            s = s_ref[sub, :]
            key_pos = off + u * _SUB + lax.broadcasted_iota(I32, (_SUB, nq), 0)
            rank = ties_before + _dot(upto[u * _SUB:(u + 1) * _SUB], tie_b)
            tie_kept = jnp.where(rank <= need, s, t_star + 1) == t_star
            bias = jnp.where(s > t_star, 0.0, jnp.where(tie_kept, 0.0, NEG_BIG))
            b_ref[sub, :] = jnp.where(key_pos <= query_pos, bias, NEG_BIG)
        return ties_before + column_sum(tie)

    lax.fori_loop(0, n_chunks, write, jnp.zeros((1, nq), F32))


_SUB = 128


def _prompt_attend_kernel(qbt_ref, qcatt_ref, wt_ref, kcat_ref, kb_ref, vbt_ref, o_ref,
                          s_ref, b_ref, m_ref, l_ref, acc_ref, *, tq, kc, topk):
    i = pl.program_id(1)
    q0 = i * tq
    n_chunks = (q0 + tq + kc - 1) // kc
    query_pos = q0 + _lane_iota((1, tq))
    w = wt_ref[...] * (IDX_HEADS ** -0.5)

    def score_chunk(c, carry):
        for u in range(kc // _SUB):
            off = pl.multiple_of(c * kc + u * _SUB, _SUB)
            s = _dot(kcat_ref[pl.ds(off, _SUB), :], qcatt_ref[...])
            acc = None
            for hd in range(IDX_HEADS):
                term = jnp.maximum(s[:, hd * tq:(hd + 1) * tq], 0.0) * w[hd:hd + 1, :]
                acc = term if acc is None else acc + term
            key_pos = off + lax.broadcasted_iota(I32, (_SUB, tq), 0)
            s_ref[pl.ds(off, _SUB), :] = jnp.where(key_pos <= query_pos, _sortable(acc + 0.0),
                                                   jnp.int32(INT_MIN))
        return carry

    lax.fori_loop(0, n_chunks, score_chunk, 0)
    _select_bias(s_ref, b_ref, n_chunks, kc, topk, query_pos)

    m_ref[...] = jnp.full(m_ref.shape, NEG_BIG, F32)
    l_ref[...] = jnp.zeros(l_ref.shape, F32)
    acc_ref[...] = jnp.zeros(acc_ref.shape, F32)
    feat = lax.broadcasted_iota(I32, (LANES, tq), 0)

    def attend_chunk(c, carry):
        off = pl.multiple_of(c * kc, kc)
        for hd in range(N_HEADS):
            pair = slice(hd // 2 * LANES, (hd // 2 + 1) * LANES)
            qp = qbt_ref[pair, :]
            qh = jnp.where((feat < HEAD_DIM) == (hd % 2 == 0), qp, jnp.zeros_like(qp))
            m, l, acc = m_ref[hd], l_ref[hd], acc_ref[hd]
            for u in range(kc // _SUB):
                ks = pl.ds(pl.multiple_of(off + u * _SUB, _SUB), _SUB)
                s = _dot(kb_ref[ks, pair], qh) + b_ref[ks, :]
                m_new = jnp.maximum(m, jnp.max(s, axis=0, keepdims=True))
                a = jnp.exp2(m - m_new)
                p = jnp.exp2(s - m_new)
                l = a * l + jnp.sum(p, axis=0, keepdims=True)
                acc = a * acc + _dot(vbt_ref[hd * HEAD_DIM:(hd + 1) * HEAD_DIM, ks], p.astype(BF16))
                m = m_new
            m_ref[hd], l_ref[hd], acc_ref[hd] = m, l, acc
        return carry

    lax.fori_loop(0, n_chunks, attend_chunk, 0)
    for hd in range(N_HEADS):
        o_ref[hd * HEAD_DIM:(hd + 1) * HEAD_DIM, :] = acc_ref[hd] / l_ref[hd]


def _prompt_attend(qb, qcat, wi, kcat, kb, vb, batch, seq):
    tq, kc = 128, min(512, seq)
    topk = min(TOPK_MAX, seq // 4)
    nq = seq // tq
    n = qb.shape[0]
    qbt, vbt, wt = qb.T, vb.T, wi[:, :_ROWS].T
    qcatt = qcat.reshape(n // tq, tq, IDX_HEADS, 2 * LANES).transpose(3, 0, 2, 1).reshape(2 * LANES, n * IDX_HEADS)
    col = lambda rows: pl.BlockSpec((rows, tq), lambda b, i: (0, b * nq + i))
    res = lambda w: pl.BlockSpec((seq, w), lambda b, i: (b, 0), pipeline_mode=pl.Buffered(1))
    res_t = pl.BlockSpec((ATTN_W, seq), lambda b, i: (0, b), pipeline_mode=pl.Buffered(1))
    return pl.pallas_call(
        functools.partial(_prompt_attend_kernel, tq=tq, kc=kc, topk=topk),
        grid=(batch, nq),
        in_specs=[col(ATTN_W), pl.BlockSpec((qcatt.shape[0], IDX_HEADS * tq), lambda b, i: (0, b * nq + i)),
                  col(wt.shape[0]), res(kcat.shape[1]), res(ATTN_W), res_t],
        out_specs=col(ATTN_W),
        out_shape=jax.ShapeDtypeStruct((ATTN_W, batch * seq), F32),
        scratch_shapes=[pltpu.VMEM((seq, tq), I32), pltpu.VMEM((seq, tq), F32),
                        pltpu.VMEM((N_HEADS, 1, tq), F32), pltpu.VMEM((N_HEADS, 1, tq), F32),
                        pltpu.VMEM((N_HEADS, HEAD_DIM, tq), F32)],
        compiler_params=pltpu.CompilerParams(vmem_limit_bytes=VMEM_LIMIT,
                                             dimension_semantics=("arbitrary", "arbitrary")),
        name="prompt_attend",
    )(qbt, qcatt, wt, kcat, kb, vbt).T


_IDX_PAGES = 16
_KV_PAGES = 16
_ROWS = 8


def _sample_select_kernel(pt_ref, q_ref, w_ref, kn_ref, *rest, n_pages, topk, idx_bits, tokens):
    pages = rest[:_IDX_PAGES]
    bias_ref, s_ref = rest[_IDX_PAGES:]
    g = pl.program_id(1)
    q1, q2, _ = _split3(q_ref[...] * (IDX_DIM ** -0.5))
    q_cat = jnp.concatenate([q1, q1, q2, jnp.zeros_like(q1)], axis=1)
    w = w_ref[...] * (IDX_HEADS ** -0.5)

    def scores(keys_t):
        k1, k2, _ = _split3(keys_t)
        s = _dot(q_cat, jnp.concatenate([k1, k2, k1, jnp.zeros_like(k1)], axis=0))
        acc = None
        for hd in range(IDX_HEADS):
            term = jnp.maximum(s[hd * _ROWS:(hd + 1) * _ROWS], 0.0) * w[:, hd:hd + 1]
            acc = term if acc is None else acc + term
        return _sortable(acc + 0.0)

    sc = scores(jnp.concatenate([pages[u][...] for u in range(_IDX_PAGES)], axis=1))
    row0 = pl.multiple_of(g * _IDX_PAGES, _IDX_PAGES)
    for t in range(tokens):
        s_ref[t, pl.ds(row0, _IDX_PAGES), :] = jnp.concatenate(
            [sc[t:t + 1, u * PAGE_SIZE:(u + 1) * PAGE_SIZE] for u in range(_IDX_PAGES)], axis=0)

    @pl.when(g == pl.num_programs(1) - 1)
    def _():
        shape = (tokens, n_pages + _ROWS, LANES)
        lane1 = _lane_iota((1, LANES))
        sn = scores(kn_ref[...])
        floor = jnp.full((_ROWS - 1, LANES), INT_MIN, I32)
        for t in range(tokens):
            new_row = jnp.where(lane1 <= t, sn[t:t + 1], jnp.int32(INT_MIN))
            s_ref[t, n_pages:, :] = jnp.concatenate([new_row, floor], axis=0)

        idx = lax.broadcasted_iota(I32, shape, 1) * LANES + _lane_iota(shape)

        def count(hit):
            c = jnp.sum(jnp.where(hit, 1.0, 0.0), axis=1, keepdims=True)
            return jnp.sum(c, axis=2, keepdims=True)

        def value_bit(it, p):
            cand = p | lax.shift_left(jnp.int32(1), 31 - it)
            c = count(s_ref[...] >= (cand ^ jnp.int32(INT_MIN)))
            return jnp.where(c >= topk, cand, p)

        p = lax.fori_loop(0, 32, value_bit, jnp.zeros((tokens, 1, 1), I32))
        t_star = p ^ jnp.int32(INT_MIN)
        need = topk - count(s_ref[...] > t_star)

        def index_bit(it, r):
            cand = r | lax.shift_left(jnp.int32(1), idx_bits - 1 - it)
            c = count(jnp.where(idx < cand, s_ref[...], jnp.int32(INT_MIN)) == t_star)
            return jnp.where(c < need, cand, r)

        j_star = lax.fori_loop(0, idx_bits, index_bit, jnp.zeros((tokens, 1, 1), I32))
        s = s_ref[...]
        tie = jnp.where(idx <= j_star, s, jnp.int32(INT_MIN)) == t_star
        bias = jnp.where(s > t_star, 0.0, jnp.where(tie, 0.0, NEG_BIG))
        bias_ref[...] = jnp.where(s == jnp.int32(INT_MIN), NEG_BIG, bias)


def _sample_select(page_table, layer, kidx_t, q32, w8, ki_new_t, tokens):
    bd, n_pages = page_table.shape
    past = n_pages * PAGE_SIZE
    topk = min(TOPK_MAX, (past + tokens) // 4)
    idx_bits = (past + LANES - 1).bit_length()
    steps = n_pages // _IDX_PAGES
    page = lambda u: pl.BlockSpec((None, None, IDX_DIM, PAGE_SIZE),
                                  lambda b, g, pt: (layer, pt[b, g * _IDX_PAGES + u], 0, 0))
    per_b = lambda *blk: pl.BlockSpec((None,) + blk, lambda b, g, pt: (b,) + (0,) * len(blk))
    plane = (tokens, n_pages + _ROWS, LANES)
    grid_spec = pltpu.PrefetchScalarGridSpec(
        num_scalar_prefetch=1,
        grid=(bd, steps),
        in_specs=[per_b(IDX_HEADS * _ROWS, IDX_DIM), per_b(_ROWS, LANES), per_b(IDX_DIM, LANES)]
        + [page(u) for u in range(_IDX_PAGES)],
        out_specs=per_b(*plane),
        scratch_shapes=[pltpu.VMEM(plane, I32)],
    )
    return pl.pallas_call(
        functools.partial(_sample_select_kernel, n_pages=n_pages, topk=topk, idx_bits=idx_bits, tokens=tokens),
        grid_spec=grid_spec,
        out_shape=jax.ShapeDtypeStruct((bd,) + plane, F32),
        compiler_params=pltpu.CompilerParams(vmem_limit_bytes=VMEM_LIMIT,
                                             dimension_semantics=("arbitrary", "arbitrary")),
        name="sample_select",
    )(page_table, q32, w8, ki_new_t, *([kidx_t] * _IDX_PAGES))


def _sample_attend_kernel(pt_ref, q_ref, bias_ref, kn_ref, vn_ref, *rest, past, groups):
    kpages = rest[:_KV_PAGES]
    vpages = rest[_KV_PAGES:2 * _KV_PAGES]
    o_ref, s_ref, acc_ref = rest[2 * _KV_PAGES:]
    g = pl.program_id(1)
    width = _KV_PAGES * PAGE_SIZE

    @pl.when(g < groups)
    def _():
        keys_t = jnp.concatenate([kpages[u][...] for u in range(_KV_PAGES)], axis=1)
        s_ref[:, pl.ds(pl.multiple_of(g * width, width), width)] = _dot(q_ref[...], keys_t)

    @pl.when(g == groups - 1)
    def _():
        s_ref[:, past:] = _dot(q_ref[...], kn_ref[...])
        bias = bias_ref[...]
        for hd in range(N_HEADS):
            rows = slice(hd * _ROWS, (hd + 1) * _ROWS)
            s = s_ref[rows, :] + bias
            p = jnp.exp2(s - jnp.max(s, axis=-1, keepdims=True))
            s_ref[rows, :] = p / jnp.sum(p, axis=-1, keepdims=True)
        acc_ref[...] = jnp.zeros(acc_ref.shape, F32)

    @pl.when(g >= groups)
    def _():
        vals_t = jnp.concatenate([vpages[u][...] for u in range(_KV_PAGES)], axis=1)
        p = s_ref[:, pl.ds(pl.multiple_of((g - groups) * width, width), width)]
        acc_ref[...] += _dot_nt(p, vals_t)

    @pl.when(g == 2 * groups - 1)
    def _():
        acc = acc_ref[...] + _dot_nt(s_ref[:, past:], vn_ref[...])
        lane_head = _lane_iota((_ROWS, ATTN_W)) // HEAD_DIM
        out = jnp.zeros((_ROWS, ATTN_W), F32)
        for hd in range(N_HEADS):
            out = out + jnp.where(lane_head == hd, acc[hd * _ROWS:(hd + 1) * _ROWS], 0.0)
        o_ref[...] = out


def _sample_attend(page_table, layer, k_t, v_t, q_bd, bias, k_new_t, v_new_t):
    bd, n_pages = page_table.shape
    past = n_pages * PAGE_SIZE
    groups = n_pages // _KV_PAGES
    kpage = lambda u: pl.BlockSpec(
        (None, None, ATTN_W, PAGE_SIZE),
        lambda b, g, pt: (layer, pt[b, jnp.minimum(g, groups - 1) * _KV_PAGES + u], 0, 0))
    vpage = lambda u: pl.BlockSpec(
        (None, None, ATTN_W, PAGE_SIZE),
        lambda b, g, pt: (layer, pt[b, jnp.maximum(g - groups, 0) * _KV_PAGES + u], 0, 0))
    per_b = lambda r, w: pl.BlockSpec((None, r, w), lambda b, g, pt: (b, 0, 0))
    rows = N_HEADS * _ROWS
    grid_spec = pltpu.PrefetchScalarGridSpec(
        num_scalar_prefetch=1,
        grid=(bd, 2 * groups),
        in_specs=[per_b(rows, ATTN_W), per_b(_ROWS, past + LANES), per_b(ATTN_W, LANES), per_b(ATTN_W, LANES)]
        + [kpage(u) for u in range(_KV_PAGES)] + [vpage(u) for u in range(_KV_PAGES)],
        out_specs=per_b(_ROWS, ATTN_W),
        scratch_shapes=[pltpu.VMEM((rows, past + LANES), F32), pltpu.VMEM((rows, ATTN_W), F32)],
    )
    return pl.pallas_call(
        functools.partial(_sample_attend_kernel, past=past, groups=groups),
        grid_spec=grid_spec,
        out_shape=jax.ShapeDtypeStruct((bd, _ROWS, ATTN_W), F32),
        compiler_params=pltpu.CompilerParams(vmem_limit_bytes=VMEM_LIMIT,
                                             dimension_semantics=("arbitrary", "arbitrary")),
        name="sample_attend",
    )(page_table, q_bd, bias, k_new_t, v_new_t, *([k_t] * _KV_PAGES), *([v_t] * _KV_PAGES))


def _ret_kernel(q_ref, k_ref, v_ref, s0_ref, o_ref, sout_ref, state_ref, *, chunk, true_len):
    c = pl.program_id(1)

    @pl.when(c == 0)
    def _():
        state_ref[...] = s0_ref[...]

    ti = lax.broadcasted_iota(I32, (chunk, chunk), 0)
    tj = lax.broadcasted_iota(I32, (chunk, chunk), 1)
    diff = (ti - tj).astype(F32)
    t = lax.broadcasted_iota(I32, (chunk, 1), 0).astype(F32)
    for hd in range(RET_HEADS):
        sl = slice(hd * RET_DK, (hd + 1) * RET_DK)
        log_g = math.log(1.0 - 2.0 ** (-5.0 - hd))
        decay = jnp.where(diff >= 0, jnp.exp(log_g * jnp.maximum(diff, 0.0)), 0.0)
        q_dec = jnp.exp(log_g * (t + 1.0))
        k_dec = jnp.exp(log_g * (true_len - 1.0 - t))
        c_dec = math.exp(log_g * true_len)
        q, k, v = q_ref[:, sl], k_ref[:, sl], v_ref[:, sl]
        s = state_ref[hd]
        att = _dot_nt(q, k) * decay
        o_ref[:, sl] = _dot(att.astype(BF16), v) + _dot(q, s.astype(BF16)) * q_dec
        kd = (k.astype(F32) * k_dec).astype(BF16)
        state_ref[hd] = s * c_dec + _dot_tn(kd, v)

    @pl.when(c == pl.num_programs(1) - 1)
    def _():
        sout_ref[...] = state_ref[...]


def _retention(rq, rk, rv, state0, chunk, true_len):
    b = state0.shape[0]
    n = rq.shape[0] // (b * chunk)
    blk = pl.BlockSpec((chunk, RET_W), lambda i, c: (i * n + c, 0))
    st = pl.BlockSpec((None, RET_HEADS, RET_DK, RET_DV), lambda i, c: (i, 0, 0, 0))
    return pl.pallas_call(
        functools.partial(_ret_kernel, chunk=chunk, true_len=true_len),
        grid=(b, n),
        in_specs=[blk, blk, blk, st],
        out_specs=[blk, st],
        out_shape=[jax.ShapeDtypeStruct(rq.shape, F32), jax.ShapeDtypeStruct(state0.shape, F32)],
        scratch_shapes=[pltpu.VMEM((RET_HEADS, RET_DK, RET_DV), F32)],
        compiler_params=pltpu.CompilerParams(vmem_limit_bytes=VMEM_LIMIT,
                                             dimension_semantics=("arbitrary", "arbitrary")),
        name="retention",
    )(rq, rk, rv, state0)


def _mix_kernel(x_ref, g_ref, sh_ref, sc_ref, gt_ref, attn_ref, ret_ref, rg_ref, gret_ref,
                wga_ref, wgr_ref, wa_ref, wr_ref, wo_ref, o_ref):
    x = x_ref[...]
    hb = _norm_mod(x, g_ref[...], sh_ref[...], sc_ref[...]).astype(BF16)
    ret = ret_ref[...]
    rg = rg_ref[...]
    parts = []
    for hd in range(RET_HEADS):
        sl = slice(hd * RET_DV, (hd + 1) * RET_DV)
        r = ret[:, sl]
        ms = jnp.mean(r * r, axis=-1, keepdims=True)
        gate = rg[:, sl]
        parts.append(r * lax.rsqrt(ms + NORM_EPS) * gret_ref[:, sl] * (gate * jax.nn.sigmoid(gate)))
    retn = jnp.concatenate(parts, axis=-1).astype(BF16)
    y_a = _dot(attn_ref[...].astype(BF16), wa_ref[...])
    y_r = _dot(retn, wr_ref[...])
    ga = jax.nn.sigmoid(_dot(hb, wga_ref[...]))
    gr = jax.nn.sigmoid(_dot(hb, wgr_ref[...]))
    mix = _dot((ga * y_a + gr * y_r).astype(BF16), wo_ref[...])
    o_ref[...] = x + gt_ref[...] * mix


def _mix(x, g, shift, scale, gate, attn, ret, rg, gret, wga, wgr, wa, wr, wo, tm):
    n, d = x.shape
    nt = n // tm
    groups, r, _ = shift.shape
    per = nt // groups if r == 1 else nt
    row = lambda w: pl.BlockSpec((tm, w), lambda i: (i, 0))
    full = lambda a: pl.BlockSpec(a.shape, lambda i: (0,) * a.ndim)
    mod = pl.BlockSpec((None, r, d), lambda i: (i // per, 0, 0))
    return pl.pallas_call(
        _mix_kernel,
        grid=(nt,),
        in_specs=[row(d), full(g), mod, mod, mod, row(ATTN_W), row(RET_W), row(RET_W), full(gret),
                  full(wga), full(wgr), full(wa), full(wr), full(wo)],
        out_specs=row(d),
        out_shape=jax.ShapeDtypeStruct((n, d), F32),
        compiler_params=pltpu.CompilerParams(vmem_limit_bytes=VMEM_LIMIT),
        name="mix",
    )(x, g, shift, scale, gate, attn, ret, rg, gret, wga, wgr, wa, wr, wo)


_ROUTER_GROUP_LANE = N_EXPERTS


def _moe_kernel(x_ref, g_ref, sh_ref, sc_ref, gt_ref, wr_ref, br_ref, w1_ref, w3_ref, w2_ref,
                o_ref, act_ref):
    tm = x_ref.shape[0]
    x = x_ref[...]
    h = _norm_mod(x, g_ref[...], sh_ref[...], sc_ref[...])
    hb = h.astype(BF16)
    logits = _dot3(_split3(h), _split3(wr_ref[...])) + br_ref[...]
    lane_f = _lane_iota((tm, LANES)).astype(F32)

    def first_lane(hit):
        return jnp.min(jnp.where(hit, lane_f, float(LANES)), axis=-1, keepdims=True)

    is_group = jnp.abs(lane_f - (_ROUTER_GROUP_LANE + (N_GROUPS - 1) / 2)) < N_GROUPS / 2
    gl = jnp.where(is_group, logits, -jnp.inf)
    gmax = jnp.max(gl, axis=-1, keepdims=True)
    g_idx = first_lane(gl == gmax) - _ROUTER_GROUP_LANE
    g_w = 1.0 / jnp.sum(jnp.exp(gl - gmax), axis=-1, keepdims=True)
    group_mid = g_idx * EXPERTS_PER_GROUP + (EXPERTS_PER_GROUP - 1) / 2
    el = jnp.where(jnp.abs(lane_f - group_mid) < EXPERTS_PER_GROUP / 2, logits, -jnp.inf)
    m1 = jnp.max(el, axis=-1, keepdims=True)
    i1 = first_lane(el == m1)
    el2 = jnp.where(lane_f == i1, -jnp.inf, el)
    m2 = jnp.max(el2, axis=-1, keepdims=True)
    i2 = first_lane(el2 == m2)
    e2 = jnp.exp(m2 - m1)
    v1 = 1.0 / (1.0 + e2)
    v2 = e2 / (1.0 + e2)
    comb = jnp.where(lane_f == i1, g_w * v1, jnp.where(lane_f == i2, g_w * v2, 0.0))

    for e in range(N_EXPERTS):
        a = _dot(hb, w1_ref[e])
        b = _dot(hb, w3_ref[e])
        act_ref[:, e * D_EXPERT:(e + 1) * D_EXPERT] = ((a * jax.nn.sigmoid(a)) * b * comb[:, e:e + 1]).astype(BF16)
    o_ref[...] = x + gt_ref[...] * _dot(act_ref[...], w2_ref[...])


def _moe(x, g, shift, scale, gate, w_route, b_route, w1, w3, w2, tm):
    n, d = x.shape
    nt = n // tm
    groups, r, _ = shift.shape
    per = nt // groups if r == 1 else nt
    row = pl.BlockSpec((tm, d), lambda i: (i, 0))
    full = lambda a: pl.BlockSpec(a.shape, lambda i: (0,) * a.ndim)
    once = lambda a: pl.BlockSpec(a.shape, lambda i: (0,) * a.ndim, pipeline_mode=pl.Buffered(1))
    mod = pl.BlockSpec((None, r, d), lambda i: (i // per, 0, 0))
    return pl.pallas_call(
        _moe_kernel,
        grid=(nt,),
        in_specs=[row, full(g), mod, mod, mod, full(w_route), full(b_route), once(w1), once(w3), once(w2)],
        out_specs=row,
        out_shape=jax.ShapeDtypeStruct((n, d), F32),
        scratch_shapes=[pltpu.VMEM((tm, w2.shape[0]), BF16)],
        compiler_params=pltpu.CompilerParams(vmem_limit_bytes=VMEM_LIMIT),
        name="moe",
    )(x, g, shift, scale, gate, w_route, b_route, w1, w3, w2)


def _rope_tables(pos, dim):
    inv = ROPE_THETA ** (-jnp.arange(0, dim, 2, dtype=F32) / dim)
    ang = pos.astype(F32)[:, None] * inv[None, :]
    cos = jnp.concatenate([jnp.cos(ang), jnp.cos(ang)], axis=-1)
    sin = jnp.concatenate([-jnp.sin(ang), jnp.sin(ang)], axis=-1)
    rep = LANES // dim
    return jnp.tile(cos, (1, rep)), jnp.tile(sin, (1, rep))


def _pack_weights(l, w_in, g_q, g_k, g_kidx, w_group, b_group, w_router, b_router):
    offs = np.cumsum((ATTN_W, ATTN_W, ATTN_W, IDX_HEADS * IDX_DIM, IDX_DIM, IDX_HEADS,
                      RET_W, RET_W, RET_W, RET_W))
    w = w_in[l]
    d = w.shape[0]
    q, k, v, qi, ki, wi, rq, rk, rv, rg = (w[:, a:b] for a, b in zip((0, *offs[:-1]), offs))
    w_gate = w[:, offs[-1]:]
    wm = jnp.concatenate([q, k, v, rq, rk, rv, rg], axis=1).astype(BF16)
    zeros = lambda n: jnp.zeros((d, n), F32)
    widx = jnp.concatenate([qi, ki, zeros(LANES - IDX_DIM), wi, zeros(LANES - IDX_HEADS)], axis=1)
    half = w_gate.shape[1] // 2
    wga, wgr = w_gate[:, :half].astype(BF16), w_gate[:, half:].astype(BF16)
    gq = jnp.tile(g_q[l], LANES // HEAD_DIM)[None, :]
    gk = jnp.tile(g_k[l], LANES // HEAD_DIM)[None, :]
    gki = jnp.concatenate([g_kidx[l], jnp.zeros((LANES - IDX_DIM,), F32)])[None, :]
    pad = LANES - N_EXPERTS - N_GROUPS
    w_route = jnp.concatenate([w_router[l], w_group[l], zeros(pad)], axis=1)
    b_route = jnp.concatenate([b_router[l], b_group[l], jnp.zeros((pad,), F32)])[None, :]
    return wm, widx, wga, wgr, gq, gk, gki, w_route, b_route


def kernel(x_prompt, x_sample, cache_k, cache_v, cache_kidx, state_ret, page_table, c_prompt, c_sample,
           w_ada, b_ada, g_mix, g_ffn, w_in, g_q, g_k, g_kidx, g_ret, w_attn_out, w_ret_out, w_o,
           w_group, b_group, w_router, b_router, w_e1, w_e3, w_e2):
    bp, seq, d = x_prompt.shape
    bd, tokens, _ = x_sample.shape
    depth = w_in.shape[0]
    n_pages = page_table.shape[1]
    past = n_pages * PAGE_SIZE
    n_p, n_s = bp * seq, bd * tokens
    assert tokens <= _ROWS and n_pages % _IDX_PAGES == 0 and seq % RET_CHUNK == 0

    tabs_p = _rope_tables(jnp.arange(seq), HEAD_DIM) + _rope_tables(jnp.arange(seq), RET_DK)
    tabs_p = tuple(jnp.tile(t, (bp, 1)) for t in tabs_p)
    pos_s = past + jnp.arange(tokens)
    tabs_s = _rope_tables(pos_s, HEAD_DIM) + _rope_tables(pos_s, RET_DK)
    tabs_s = tuple(jnp.tile(t, (bd, 1)) for t in tabs_s)

    c_all = jnp.concatenate([c_prompt, c_sample], axis=0)
    c_rows = -(-c_all.shape[0] // 8) * 8
    c_all = jnp.pad(c_all, ((0, c_rows - c_all.shape[0]), (0, 0)))
    kidx_t = cache_kidx.transpose(0, 1, 3, 2)
    k_t = cache_k.transpose(0, 1, 3, 4, 2).reshape(cache_k.shape[:2] + (ATTN_W, PAGE_SIZE))
    v_t = cache_v.transpose(0, 1, 3, 4, 2).reshape(cache_v.shape[:2] + (ATTN_W, PAGE_SIZE))
    ret_pad = 16
    tm_p = 256 if seq % 256 == 0 else 128
    tm_moe = 512 if seq % 512 == 0 else tm_p

    y_p = x_prompt.reshape(n_p, d)
    y_s = x_sample.reshape(n_s, d)
    outs = {name: [] for name in ("kp", "vp", "kip", "sp", "ks", "vs", "kis", "ss")}
    for l in range(depth):
        wm, widx, wga, wgr, gq, gk, gki, w_route, b_route = _pack_weights(
            l, w_in, g_q, g_k, g_kidx, w_group, b_group, w_router, b_router)
        wa, wr, wo = w_attn_out[l].astype(BF16), w_ret_out[l].astype(BF16), w_o[l].astype(BF16)
        w1, w3 = w_e1[l].astype(BF16), w_e3[l].astype(BF16)
        w2 = w_e2[l].astype(BF16).reshape(N_EXPERTS * D_EXPERT, d)
        gmix, gffn, gret = g_mix[l][None, :], g_ffn[l][None, :], g_ret[l][None, :]

        ada = _ada(c_all, w_ada[l], b_ada[l])
        mods_p = [m[:bp, None, :] for m in jnp.split(ada, 6, axis=-1)]
        mods_s = [jnp.repeat(m[bp:bp + bd], tokens, axis=0)[None] for m in jnp.split(ada, 6, axis=-1)]

        sh_m, sc_m, gt_m, sh_f, sc_f, gt_f = mods_p
        (qb, k, kb, v, vb, qi, qcat, ki, kcat, wi, rq, rk, rv, rg) = _proj(
            y_p, gmix, sh_m, sc_m, tabs_p, wm, widx, gq, gk, gki, tm_p)
        attn = _prompt_attend(qb, qcat, wi, kcat, kb, vb, bp, seq)
        ret, s_fin = _retention(rq, rk, rv, jnp.zeros((bp, RET_HEADS, RET_DK, RET_DV), F32),
                                RET_CHUNK, RET_CHUNK)
        y_p = _mix(y_p, gmix, sh_m, sc_m, gt_m, attn, ret, rg, gret, wga, wgr, wa, wr, wo, tm_p)
        y_p = _moe(y_p, gffn, sh_f, sc_f, gt_f, w_route, b_route, w1, w3, w2, tm_moe)
        outs["kp"].append(k.reshape(bp, seq, N_HEADS, HEAD_DIM))
        outs["vp"].append(v.reshape(bp, seq, N_HEADS, HEAD_DIM))
        outs["kip"].append(ki.reshape(bp, seq, IDX_DIM))
        outs["sp"].append(s_fin)

        sh_m, sc_m, gt_m, sh_f, sc_f, gt_f = mods_s
        (qb, k, kb, v, vb, qi, qcat, ki, kcat, wi, rq, rk, rv, rg) = _proj(
            y_s, gmix, sh_m, sc_m, tabs_s, wm, widx, gq, gk, gki, n_s)
        pad_to = lambda a, rows: jnp.pad(a, ((0, 0), (0, rows - a.shape[1])) + ((0, 0),) * (a.ndim - 2))
        q32 = pad_to(qi.reshape(bd, tokens, IDX_HEADS, IDX_DIM), _ROWS).transpose(0, 2, 1, 3)
        q32 = q32.reshape(bd, IDX_HEADS * _ROWS, IDX_DIM)
        w8 = pad_to(wi.reshape(bd, tokens, LANES), _ROWS)
        new_t = lambda a: pad_to(a.reshape(bd, tokens, -1), LANES).transpose(0, 2, 1)
        plane = _sample_select(page_table, l, kidx_t, q32, w8, new_t(ki), tokens)
        bias = pad_to(plane[:, :, :n_pages + 1].reshape(bd, tokens, past + LANES), _ROWS)
        qh = pad_to(qb.astype(F32).reshape(bd, tokens, N_HEADS, HEAD_DIM), _ROWS).transpose(0, 2, 1, 3)
        q_bd = jnp.einsum('bhtd,hg->bhtgd', qh, jnp.eye(N_HEADS, dtype=F32)).reshape(bd, N_HEADS * _ROWS, ATTN_W)
        attn = _sample_attend(page_table, l, k_t, v_t, q_bd, bias, new_t(k), new_t(v))
        attn = attn[:, :tokens].reshape(n_s, ATTN_W)
        pad_chunk = lambda a: jnp.pad(a.reshape(bd, tokens, RET_W),
                                      ((0, 0), (0, ret_pad - tokens), (0, 0))).reshape(bd * ret_pad, RET_W)
        ret, s_fin = _retention(pad_chunk(rq), pad_chunk(rk), pad_chunk(rv), state_ret[l], ret_pad, tokens)
        ret = ret.reshape(bd, ret_pad, RET_W)[:, :tokens].reshape(n_s, RET_W)
        y_s = _mix(y_s, gmix, sh_m, sc_m, gt_m, attn, ret, rg, gret, wga, wgr, wa, wr, wo, n_s)
        y_s = _moe(y_s, gffn, sh_f, sc_f, gt_f, w_route, b_route, w1, w3, w2, n_s)
        outs["ks"].append(k.reshape(bd, tokens, N_HEADS, HEAD_DIM))
        outs["vs"].append(v.reshape(bd, tokens, N_HEADS, HEAD_DIM))
        outs["kis"].append(ki.reshape(bd, tokens, IDX_DIM))
        outs["ss"].append(s_fin)

    st = lambda name: jnp.stack(outs[name])
    return (y_p.reshape(bp, seq, d), y_s.reshape(bd, tokens, d),
            st("kp"), st("vp"), st("kip"), st("sp"), st("ks"), st("vs"), st("kis"), st("ss"))
```

```python
import functools
import math

import numpy as np
import jax
import jax.numpy as jnp
from jax import lax
from jax.experimental import pallas as pl
from jax.experimental.pallas import tpu as pltpu

N_HEADS = 8
HEAD_DIM = 64
IDX_HEADS = 4
IDX_DIM = 64
TOPK_MAX = 256
PAGE_SIZE = 128
RET_HEADS = 4
RET_DK = 128
RET_DV = 128
RET_CHUNK = 128
N_GROUPS = 4
EXPERTS_PER_GROUP = 4
N_EXPERTS = N_GROUPS * EXPERTS_PER_GROUP
D_EXPERT = 256
ROPE_THETA = 10000.0
NORM_EPS = 1e-6
ATTN_W = N_HEADS * HEAD_DIM
RET_W = RET_HEADS * RET_DK

LANES = 128
VMEM_LIMIT = 56 * 1024 * 1024
NEG_BIG = -1e30
_Q_SCALE = HEAD_DIM ** -0.5 * math.log2(math.e)
INT_MIN = -2 ** 31

F32 = jnp.float32
BF16 = jnp.bfloat16
I32 = jnp.int32


def _dot(a, b):
    return jnp.dot(a, b, preferred_element_type=F32)


def _dot_nt(a, b):
    return lax.dot_general(a, b, (((1,), (1,)), ((), ())), preferred_element_type=F32)


def _dot_tn(a, b):
    return lax.dot_general(a, b, (((0,), (0,)), ((), ())), preferred_element_type=F32)


def _split3(a):
    a1 = a.astype(BF16)
    r1 = a - a1.astype(F32)
    a2 = r1.astype(BF16)
    a3 = (r1 - a2.astype(F32)).astype(BF16)
    return a1, a2, a3


def _dot3(a3, b3, dot=_dot, terms=6):
    a1, a2, a_3 = a3
    b1, b2, b_3 = b3
    mid = dot(a1, b2) + dot(a2, b1)
    if terms == 6:
        mid = mid + (dot(a1, b_3) + dot(a_3, b1) + dot(a2, b2))
    return dot(a1, b1) + mid


def _norm_mod(x, g, shift, scale):
    ms = jnp.mean(x * x, axis=-1, keepdims=True)
    y = x * lax.rsqrt(ms + NORM_EPS) * g
    return y * (1.0 + scale) + shift


def _lane_iota(shape):
    return lax.broadcasted_iota(I32, shape, len(shape) - 1)


def _rope(x, cos, sin, half):
    if 2 * half == LANES:
        swapped = pltpu.roll(x, half, 1)
    else:
        first = (_lane_iota(x.shape) & (2 * half - 1)) < half
        swapped = jnp.where(first, pltpu.roll(x, LANES - half, 1), pltpu.roll(x, half, 1))
    return x * cos + swapped * sin


def _group_ones(width):
    r = lax.broadcasted_iota(I32, (LANES, LANES), 0) // width
    c = lax.broadcasted_iota(I32, (LANES, LANES), 1) // width
    return jnp.where(r == c, 1.0, 0.0).astype(BF16)


def _group_mean_sq(x, ones, width):
    y1, y2, y3 = _split3(x * x)
    return (_dot(y1, ones) + (_dot(y2, ones) + _dot(y3, ones))) * (1.0 / width)


def _sortable(x):
    bits = lax.bitcast_convert_type(x, I32)
    return bits ^ ((bits >> 31) & jnp.int32(0x7FFFFFFF))


def _ada_kernel(c_ref, w_ref, b_ref, o_ref):
    c = c_ref[...]
    s = c * jax.nn.sigmoid(c)
    o_ref[...] = _dot3(_split3(s), _split3(w_ref[...])) + b_ref[...]


def _ada(c_all, w_ada, b_ada):
    m, d = c_all.shape
    n = w_ada.shape[1]
    tn = 1024
    return pl.pallas_call(
        _ada_kernel,
        grid=(n // tn,),
        in_specs=[pl.BlockSpec((m, d), lambda j: (0, 0)),
                  pl.BlockSpec((d, tn), lambda j: (0, j)),
                  pl.BlockSpec((1, tn), lambda j: (0, j))],
        out_specs=pl.BlockSpec((m, tn), lambda j: (0, j)),
        out_shape=jax.ShapeDtypeStruct((m, n), F32),
        compiler_params=pltpu.CompilerParams(vmem_limit_bytes=VMEM_LIMIT),
        name="ada",
    )(c_all, w_ada, b_ada.reshape(1, n))


_OFF_Q, _OFF_K, _OFF_V, _OFF_RQ, _OFF_RK, _OFF_RV, _OFF_RG = (0, 512, 1024, 1536, 2048, 2560, 3072)
_W_MAIN = 3584
_W_IDX = 512


def _split_concat(x, hi_half):
    x1, x2, _ = _split3(x)
    x1, x2 = x1.astype(F32), x2.astype(F32)
    lo = _lane_iota(x.shape) < LANES // 2
    r1, r2 = pltpu.roll(x1, LANES // 2, 1), pltpu.roll(x2, LANES // 2, 1)
    if hi_half:
        return jnp.where(lo, r1, x1), jnp.where(lo, r2, 0.0)
    return jnp.where(lo, x1, r1), jnp.where(lo, x2, 0.0)


_TQ = 128


def _proj_kernel(x_ref, g_ref, sh_ref, sc_ref, c64_ref, s64_ref, c128_ref, s128_ref,
                 wm_ref, wi_ref, gq_ref, gk_ref, gki_ref,
                 qb_ref, k_ref, kb_ref, v_ref, qi_ref, ki_ref, kcat_ref, wo_ref, rq_ref, rk_ref, rv_ref, rg_ref,
                 qbt_ref, kt_ref, vt_ref, vbt_ref, qcatt_ref, wt_ref):
    tm = x_ref.shape[0]
    h = _norm_mod(x_ref[...], g_ref[...], sh_ref[...], sc_ref[...])
    hb = h.astype(BF16)
    c64, s64 = c64_ref[...], s64_ref[...]
    c128, s128 = c128_ref[...], s128_ref[...]
    ones64 = _group_ones(HEAD_DIM)
    gq, gk = gq_ref[...], gk_ref[...]

    def seg(off, width):
        return _dot(hb, wm_ref[:, off:off + width])

    zq, zk, zv = seg(_OFF_Q, ATTN_W), seg(_OFF_K, ATTN_W), seg(_OFF_V, ATTN_W)
    for j in range(ATTN_W // LANES):
        sl = slice(j * LANES, (j + 1) * LANES)
        q = zq[:, sl]
        q = q * lax.rsqrt(_group_mean_sq(q, ones64, HEAD_DIM) + NORM_EPS) * gq
        q = _rope(q, c64, s64, HEAD_DIM // 2) * _Q_SCALE
        qb_ref[:, sl] = q.astype(BF16)
        qbt_ref[sl, :] = q.T.astype(BF16)
        k = zk[:, sl]
        k = k * lax.rsqrt(_group_mean_sq(k, ones64, HEAD_DIM) + NORM_EPS) * gk
        k = _rope(k, c64, s64, HEAD_DIM // 2)
        k_ref[:, sl] = k
        kb_ref[:, sl] = k.astype(BF16)
        kt_ref[sl, :] = k.T
        v = zv[:, sl]
        v_ref[:, sl] = v
        vt = v.T
        vt_ref[sl, :] = vt
        vbt_ref[sl, :] = vt.astype(BF16)
    zrq, zrk = seg(_OFF_RQ, RET_W), seg(_OFF_RK, RET_W)
    rv_ref[...] = seg(_OFF_RV, RET_W).astype(BF16)
    rg_ref[...] = seg(_OFF_RG, RET_W)
    for j in range(RET_W // LANES):
        sl = slice(j * LANES, (j + 1) * LANES)
        rq_ref[:, sl] = _rope(zrq[:, sl], c128, s128, RET_DK // 2).astype(BF16)
        rk_ref[:, sl] = (_rope(zrk[:, sl], c128, s128, RET_DK // 2) * (RET_DK ** -0.5)).astype(BF16)

    zi = _dot3(_split3(h), _split3(wi_ref[...]), terms=3)
    for j in range(2):
        sl = slice(j * LANES, (j + 1) * LANES)
        qi = _rope(zi[:, sl], c64, s64, IDX_DIM // 2)
        qi_ref[:, sl] = qi
        for hi_half in (False, True):
            hd = 2 * j + hi_half
            for part, rows in zip(_split_concat(qi * (IDX_DIM ** -0.5), hi_half), (slice(0, LANES), slice(LANES, 2 * LANES))):
                part_t = part.T.astype(BF16)
                for blk in range(tm // _TQ):
                    col = (blk * IDX_HEADS + hd) * _TQ
                    qcatt_ref[rows, col:col + _TQ] = part_t[:, blk * _TQ:(blk + 1) * _TQ]
    zk = zi[:, 2 * LANES: 3 * LANES]
    ms = jnp.sum(zk * zk, axis=-1, keepdims=True) * (1.0 / IDX_DIM)
    ki = _rope(zk * lax.rsqrt(ms + NORM_EPS) * gki_ref[...], c64, s64, IDX_DIM // 2)
    ki_ref[...] = ki[:, :IDX_DIM]
    k1, k2, _ = _split3(ki)
    k1, k2 = k1.astype(F32), k2.astype(F32)
    kcat_ref[:, :LANES] = (k1 + pltpu.roll(k2, LANES // 2, 1)).astype(BF16)
    kcat_ref[:, LANES:] = k1.astype(BF16)
    wo = zi[:, 3 * LANES: 4 * LANES]
    wo_ref[...] = wo
    wt_ref[...] = wo.T[:_ROWS, :]


def _proj(x, g, shift, scale, tabs, wm, wi, gq, gk, gki, tm):
    n, d = x.shape
    nt = n // tm
    assert tm % _TQ == 0
    groups, r, _ = shift.shape
    per = nt // groups if r == 1 else nt
    row = lambda w: pl.BlockSpec((tm, w), lambda i: (i, 0))
    nb = groups if r == 1 else 1
    col = lambda rows, mult=1: pl.BlockSpec((None, rows, tm * mult), lambda i: (i // (nt // nb), 0, i % (nt // nb)))
    full = lambda a: pl.BlockSpec(a.shape, lambda i: (0,) * a.ndim)
    mod = pl.BlockSpec((None, r, d), lambda i: (i // per, 0, 0))
    c64, s64, c128, s128 = tabs
    outs = [(ATTN_W, BF16), (ATTN_W, F32), (ATTN_W, BF16), (ATTN_W, F32),
            (IDX_HEADS * IDX_DIM, F32), (IDX_DIM, F32), (2 * LANES, BF16), (LANES, F32),
            (RET_W, BF16), (RET_W, BF16), (RET_W, BF16), (RET_W, F32)]
    outs_t = [(ATTN_W, 1, BF16), (ATTN_W, 1, F32), (ATTN_W, 1, F32), (ATTN_W, 1, BF16),
              (2 * LANES, IDX_HEADS, BF16), (_ROWS, 1, F32)]
    return pl.pallas_call(
        _proj_kernel,
        grid=(nt,),
        in_specs=[row(d), full(g), mod, mod, row(LANES), row(LANES), row(LANES), row(LANES),
                  full(wm), full(wi), full(gq), full(gk), full(gki)],
        out_specs=[row(w) for w, _ in outs] + [col(rows, mult) for rows, mult, _ in outs_t],
        out_shape=[jax.ShapeDtypeStruct((n, w), dt) for w, dt in outs]
        + [jax.ShapeDtypeStruct((nb, rows, n // nb * mult), dt) for rows, mult, dt in outs_t],
        compiler_params=pltpu.CompilerParams(vmem_limit_bytes=VMEM_LIMIT),
        name="proj",
    )(x, g, shift, scale, c64, s64, c128, s128, wm, wi, gq, gk, gki)


def _select_bias(s_ref, b_ref, n_chunks, kc, topk, query_pos):
    nq = s_ref.shape[1]
    fold = 64

    def partial_sums(x):
        parts = [x[j * fold:(j + 1) * fold] for j in range(kc // fold)]
        while len(parts) > 1:
            parts = [a + b for a, b in zip(parts[::2], parts[1::2])] + parts[len(parts) & ~1:]
        return parts[0]

    def column_sum(x):
        return jnp.sum(partial_sums(x), axis=0, keepdims=True)

    def count(pred):
        def body(c, cnt):
            off = pl.multiple_of(c * kc, kc)
            return cnt + partial_sums(jnp.where(pred(s_ref[pl.ds(off, kc), :]), 1.0, 0.0))
        cnt = lax.fori_loop(0, n_chunks, body, jnp.zeros((fold, nq), F32))
        return jnp.sum(cnt, axis=0, keepdims=True)

    def value_bit(it, p):
        cand = p | lax.shift_left(jnp.int32(1), 31 - it)
        t = cand ^ jnp.int32(INT_MIN)
        c = count(lambda s: s >= t)
        return jnp.where(c >= topk, cand, p)

    p = lax.fori_loop(0, 32, value_bit, jnp.zeros((1, nq), I32))
    t_star = p ^ jnp.int32(INT_MIN)
    need = topk - count(lambda s: s > t_star)

    upto = jnp.where(lax.broadcasted_iota(I32, (kc, kc), 1) <= lax.broadcasted_iota(I32, (kc, kc), 0),
                     1.0, 0.0).astype(BF16)

    def write(c, ties_before):
        off = pl.multiple_of(c * kc, kc)
        tie = jnp.where(s_ref[pl.ds(off, kc), :] == t_star, 1.0, 0.0)
        tie_b = tie.astype(BF16)
        for u in range(kc // _SUB):
            sub = pl.ds(pl.multiple_of(off + u * _SUB, _SUB), _SUB)
            s = s_ref[sub, :]
            key_pos = off + u * _SUB + lax.broadcasted_iota(I32, (_SUB, nq), 0)
            rank = ties_before + _dot(upto[u * _SUB:(u + 1) * _SUB], tie_b)
            tie_kept = jnp.where(rank <= need, s, t_star + 1) == t_star
            bias = jnp.where(s > t_star, 0.0, jnp.where(tie_kept, 0.0, NEG_BIG))
            b_ref[sub, :] = jnp.where(key_pos <= query_pos, bias, NEG_BIG)
        return ties_before + column_sum(tie)

    lax.fori_loop(0, n_chunks, write, jnp.zeros((1, nq), F32))


_SUB = 128


def _prompt_attend_kernel(qbt_ref, qcatt_ref, wt_ref, kcat_ref, kb_ref, vbt_ref, o_ref,
                          s_ref, b_ref, m_ref, l_ref, acc_ref, *, tq, kc, topk):
    i = pl.program_id(1)
    q0 = i * tq
    n_chunks = (q0 + tq + kc - 1) // kc
    query_pos = q0 + _lane_iota((1, tq))
    w = wt_ref[...] * (IDX_HEADS ** -0.5)

    def score_chunk(c, carry):
        for u in range(kc // _SUB):
            off = pl.multiple_of(c * kc + u * _SUB, _SUB)
            s = _dot(kcat_ref[pl.ds(off, _SUB), :], qcatt_ref[...])
            acc = None
            for hd in range(IDX_HEADS):
                term = jnp.maximum(s[:, hd * tq:(hd + 1) * tq], 0.0) * w[hd:hd + 1, :]
                acc = term if acc is None else acc + term
            key_pos = off + lax.broadcasted_iota(I32, (_SUB, tq), 0)
            s_ref[pl.ds(off, _SUB), :] = jnp.where(key_pos <= query_pos, _sortable(acc + 0.0),
                                                   jnp.int32(INT_MIN))
        return carry

    lax.fori_loop(0, n_chunks, score_chunk, 0)
    _select_bias(s_ref, b_ref, n_chunks, kc, topk, query_pos)

    m_ref[...] = jnp.full(m_ref.shape, NEG_BIG, F32)
    l_ref[...] = jnp.zeros(l_ref.shape, F32)
    acc_ref[...] = jnp.zeros(acc_ref.shape, F32)
    feat = lax.broadcasted_iota(I32, (LANES, tq), 0)

    def attend_chunk(c, carry):
        off = pl.multiple_of(c * kc, kc)
        for hd in range(N_HEADS):
            pair = slice(hd // 2 * LANES, (hd // 2 + 1) * LANES)
            qp = qbt_ref[pair, :]
            qh = jnp.where((feat < HEAD_DIM) == (hd % 2 == 0), qp, jnp.zeros_like(qp))
            m, l, acc = m_ref[hd], l_ref[hd], acc_ref[hd]
            for u in range(kc // _SUB):
                ks = pl.ds(pl.multiple_of(off + u * _SUB, _SUB), _SUB)
                s = _dot(kb_ref[ks, pair], qh) + b_ref[ks, :]
                m_new = jnp.maximum(m, jnp.max(s, axis=0, keepdims=True))
                a = jnp.exp2(m - m_new)
                p = jnp.exp2(s - m_new)
                l = a * l + jnp.sum(p, axis=0, keepdims=True)
                acc = a * acc + _dot(vbt_ref[hd * HEAD_DIM:(hd + 1) * HEAD_DIM, ks], p.astype(BF16))
                m = m_new
            m_ref[hd], l_ref[hd], acc_ref[hd] = m, l, acc
        return carry

    lax.fori_loop(0, n_chunks, attend_chunk, 0)
    for hd in range(N_HEADS):
        o_ref[hd * HEAD_DIM:(hd + 1) * HEAD_DIM, :] = acc_ref[hd] / l_ref[hd]


def _prompt_attend(qbt, qcatt, wt, kcat, kb, vbt, batch, seq):
    tq, kc = _TQ, min(512, seq)
    topk = min(TOPK_MAX, seq // 4)
    nq = seq // tq
    col = lambda rows, mult=1: pl.BlockSpec((None, rows, tq * mult), lambda b, i: (b, 0, i))
    res = lambda w: pl.BlockSpec((seq, w), lambda b, i: (b, 0), pipeline_mode=pl.Buffered(1))
    res_t = pl.BlockSpec((None, ATTN_W, seq), lambda b, i: (b, 0, 0), pipeline_mode=pl.Buffered(1))
    return pl.pallas_call(
        functools.partial(_prompt_attend_kernel, tq=tq, kc=kc, topk=topk),
        grid=(batch, nq),
        in_specs=[col(ATTN_W), col(qcatt.shape[1], IDX_HEADS), col(wt.shape[1]),
                  res(kcat.shape[1]), res(ATTN_W), res_t],
        out_specs=col(ATTN_W),
        out_shape=jax.ShapeDtypeStruct((batch, ATTN_W, seq), F32),
        scratch_shapes=[pltpu.VMEM((seq, tq), I32), pltpu.VMEM((seq, tq), F32),
                        pltpu.VMEM((N_HEADS, 1, tq), F32), pltpu.VMEM((N_HEADS, 1, tq), F32),
                        pltpu.VMEM((N_HEADS, HEAD_DIM, tq), F32)],
        compiler_params=pltpu.CompilerParams(vmem_limit_bytes=VMEM_LIMIT,
                                             dimension_semantics=("arbitrary", "arbitrary")),
        name="prompt_attend",
    )(qbt, qcatt, wt, kcat, kb, vbt)


_IDX_PAGES = 16
_KV_PAGES = 16
_ROWS = 8


def _sample_select_kernel(pt_ref, q_ref, w_ref, kn_ref, *rest, n_pages, topk, idx_bits, tokens):
    pages = rest[:_IDX_PAGES]
    bias_ref, s_ref = rest[_IDX_PAGES:]
    g = pl.program_id(1)
    q1, q2, _ = _split3(q_ref[...] * (IDX_DIM ** -0.5))
    q_cat = jnp.concatenate([q1, q1, q2, jnp.zeros_like(q1)], axis=1)
    w = w_ref[...] * (IDX_HEADS ** -0.5)

    def scores(keys_t):
        k1, k2, _ = _split3(keys_t)
        s = _dot(q_cat, jnp.concatenate([k1, k2, k1, jnp.zeros_like(k1)], axis=0))
        acc = None
        for hd in range(IDX_HEADS):
            term = jnp.maximum(s[hd * _ROWS:(hd + 1) * _ROWS], 0.0) * w[:, hd:hd + 1]
            acc = term if acc is None else acc + term
        return _sortable(acc + 0.0)

    sc = scores(jnp.concatenate([pages[u][...] for u in range(_IDX_PAGES)], axis=1))
    row0 = pl.multiple_of(g * _IDX_PAGES, _IDX_PAGES)
    for t in range(tokens):
        s_ref[t, pl.ds(row0, _IDX_PAGES), :] = jnp.concatenate(
            [sc[t:t + 1, u * PAGE_SIZE:(u + 1) * PAGE_SIZE] for u in range(_IDX_PAGES)], axis=0)

    @pl.when(g == pl.num_programs(1) - 1)
    def _():
        shape = (tokens, n_pages + _ROWS, LANES)
        lane1 = _lane_iota((1, LANES))
        sn = scores(kn_ref[...])
        floor = jnp.full((_ROWS - 1, LANES), INT_MIN, I32)
        for t in range(tokens):
            new_row = jnp.where(lane1 <= t, sn[t:t + 1], jnp.int32(INT_MIN))
            s_ref[t, n_pages:, :] = jnp.concatenate([new_row, floor], axis=0)

        idx = lax.broadcasted_iota(I32, shape, 1) * LANES + _lane_iota(shape)

        def count(hit):
            c = jnp.sum(jnp.where(hit, 1.0, 0.0), axis=1, keepdims=True)
            return jnp.sum(c, axis=2, keepdims=True)

        def value_bit(it, p):
            cand = p | lax.shift_left(jnp.int32(1), 31 - it)
            c = count(s_ref[...] >= (cand ^ jnp.int32(INT_MIN)))
            return jnp.where(c >= topk, cand, p)

        p = lax.fori_loop(0, 32, value_bit, jnp.zeros((tokens, 1, 1), I32))
        t_star = p ^ jnp.int32(INT_MIN)
        need = topk - count(s_ref[...] > t_star)

        def index_bit(it, r):
            cand = r | lax.shift_left(jnp.int32(1), idx_bits - 1 - it)
            c = count(jnp.where(idx < cand, s_ref[...], jnp.int32(INT_MIN)) == t_star)
            return jnp.where(c < need, cand, r)

        j_star = lax.fori_loop(0, idx_bits, index_bit, jnp.zeros((tokens, 1, 1), I32))
        s = s_ref[...]
        tie = jnp.where(idx <= j_star, s, jnp.int32(INT_MIN)) == t_star
        bias = jnp.where(s > t_star, 0.0, jnp.where(tie, 0.0, NEG_BIG))
        bias_ref[...] = jnp.where(s == jnp.int32(INT_MIN), NEG_BIG, bias)


def _sample_select(page_table, layer, kidx_t, q32, w8, ki_new_t, tokens):
    bd, n_pages = page_table.shape
    past = n_pages * PAGE_SIZE
    topk = min(TOPK_MAX, (past + tokens) // 4)
    idx_bits = (past + LANES - 1).bit_length()
    steps = n_pages // _IDX_PAGES
    page = lambda u: pl.BlockSpec((None, None, IDX_DIM, PAGE_SIZE),
                                  lambda b, g, pt: (layer, pt[b, g * _IDX_PAGES + u], 0, 0))
    per_b = lambda *blk: pl.BlockSpec((None,) + blk, lambda b, g, pt: (b,) + (0,) * len(blk))
    plane = (tokens, n_pages + _ROWS, LANES)
    grid_spec = pltpu.PrefetchScalarGridSpec(
        num_scalar_prefetch=1,
        grid=(bd, steps),
        in_specs=[per_b(IDX_HEADS * _ROWS, IDX_DIM), per_b(_ROWS, LANES), per_b(IDX_DIM, LANES)]
        + [page(u) for u in range(_IDX_PAGES)],
        out_specs=per_b(*plane),
        scratch_shapes=[pltpu.VMEM(plane, I32)],
    )
    return pl.pallas_call(
        functools.partial(_sample_select_kernel, n_pages=n_pages, topk=topk, idx_bits=idx_bits, tokens=tokens),
        grid_spec=grid_spec,
        out_shape=jax.ShapeDtypeStruct((bd,) + plane, F32),
        compiler_params=pltpu.CompilerParams(vmem_limit_bytes=VMEM_LIMIT,
                                             dimension_semantics=("arbitrary", "arbitrary")),
        name="sample_select",
    )(page_table, q32, w8, ki_new_t, *([kidx_t] * _IDX_PAGES))


def _sample_attend_kernel(pt_ref, q_ref, bias_ref, kn_ref, vn_ref, *rest, past, groups):
    kpages = rest[:_KV_PAGES]
    vpages = rest[_KV_PAGES:2 * _KV_PAGES]
    o_ref, s_ref, acc_ref = rest[2 * _KV_PAGES:]
    g = pl.program_id(1)
    width = _KV_PAGES * PAGE_SIZE

    @pl.when(g < groups)
    def _():
        keys_t = jnp.concatenate([kpages[u][...] for u in range(_KV_PAGES)], axis=1)
        s_ref[:, pl.ds(pl.multiple_of(g * width, width), width)] = _dot(q_ref[...], keys_t)

    @pl.when(g == groups - 1)
    def _():
        s_ref[:, past:] = _dot(q_ref[...], kn_ref[...])
        bias = bias_ref[...]
        for hd in range(N_HEADS):
            rows = slice(hd * _ROWS, (hd + 1) * _ROWS)
            s = s_ref[rows, :] + bias
            p = jnp.exp2(s - jnp.max(s, axis=-1, keepdims=True))
            s_ref[rows, :] = p / jnp.sum(p, axis=-1, keepdims=True)
        acc_ref[...] = jnp.zeros(acc_ref.shape, F32)

    @pl.when(g >= groups)
    def _():
        vals_t = jnp.concatenate([vpages[u][...] for u in range(_KV_PAGES)], axis=1)
        p = s_ref[:, pl.ds(pl.multiple_of((g - groups) * width, width), width)]
        acc_ref[...] += _dot_nt(p, vals_t)

    @pl.when(g == 2 * groups - 1)
    def _():
        acc = acc_ref[...] + _dot_nt(s_ref[:, past:], vn_ref[...])
        lane_head = _lane_iota((_ROWS, ATTN_W)) // HEAD_DIM
        out = jnp.zeros((_ROWS, ATTN_W), F32)
        for hd in range(N_HEADS):
            out = out + jnp.where(lane_head == hd, acc[hd * _ROWS:(hd + 1) * _ROWS], 0.0)
        o_ref[...] = out


def _sample_attend(page_table, layer, k_t, v_t, q_bd, bias, k_new_t, v_new_t):
    bd, n_pages = page_table.shape
    past = n_pages * PAGE_SIZE
    groups = n_pages // _KV_PAGES
    kpage = lambda u: pl.BlockSpec(
        (None, None, ATTN_W, PAGE_SIZE),
        lambda b, g, pt: (layer, pt[b, jnp.minimum(g, groups - 1) * _KV_PAGES + u], 0, 0))
    vpage = lambda u: pl.BlockSpec(
        (None, None, ATTN_W, PAGE_SIZE),
        lambda b, g, pt: (layer, pt[b, jnp.maximum(g - groups, 0) * _KV_PAGES + u], 0, 0))
    per_b = lambda r, w: pl.BlockSpec((None, r, w), lambda b, g, pt: (b, 0, 0))
    rows = N_HEADS * _ROWS
    grid_spec = pltpu.PrefetchScalarGridSpec(
        num_scalar_prefetch=1,
        grid=(bd, 2 * groups),
        in_specs=[per_b(rows, ATTN_W), per_b(_ROWS, past + LANES), per_b(ATTN_W, LANES), per_b(ATTN_W, LANES)]
        + [kpage(u) for u in range(_KV_PAGES)] + [vpage(u) for u in range(_KV_PAGES)],
        out_specs=per_b(_ROWS, ATTN_W),
        scratch_shapes=[pltpu.VMEM((rows, past + LANES), F32), pltpu.VMEM((rows, ATTN_W), F32)],
    )
    return pl.pallas_call(
        functools.partial(_sample_attend_kernel, past=past, groups=groups),
        grid_spec=grid_spec,
        out_shape=jax.ShapeDtypeStruct((bd, _ROWS, ATTN_W), F32),
        compiler_params=pltpu.CompilerParams(vmem_limit_bytes=VMEM_LIMIT,
                                             dimension_semantics=("arbitrary", "arbitrary")),
        name="sample_attend",
    )(page_table, q_bd, bias, k_new_t, v_new_t, *([k_t] * _KV_PAGES), *([v_t] * _KV_PAGES))


def _ret_kernel(q_ref, k_ref, v_ref, s0_ref, o_ref, sout_ref, state_ref, *, chunk, true_len):
    c = pl.program_id(1)

    @pl.when(c == 0)
    def _():
        state_ref[...] = s0_ref[...]

    ti = lax.broadcasted_iota(I32, (chunk, chunk), 0)
    tj = lax.broadcasted_iota(I32, (chunk, chunk), 1)
    diff = (ti - tj).astype(F32)
    t = lax.broadcasted_iota(I32, (chunk, 1), 0).astype(F32)
    for hd in range(RET_HEADS):
        sl = slice(hd * RET_DK, (hd + 1) * RET_DK)
        log_g = math.log(1.0 - 2.0 ** (-5.0 - hd))
        decay = jnp.where(diff >= 0, jnp.exp(log_g * jnp.maximum(diff, 0.0)), 0.0)
        q_dec = jnp.exp(log_g * (t + 1.0))
        k_dec = jnp.exp(log_g * (true_len - 1.0 - t))
        c_dec = math.exp(log_g * true_len)
        q, k, v = q_ref[:, sl], k_ref[:, sl], v_ref[:, sl]
        s = state_ref[hd]
        att = _dot_nt(q, k) * decay
        o_ref[:, sl] = _dot(att.astype(BF16), v) + _dot(q, s.astype(BF16)) * q_dec
        kd = (k.astype(F32) * k_dec).astype(BF16)
        state_ref[hd] = s * c_dec + _dot_tn(kd, v)

    @pl.when(c == pl.num_programs(1) - 1)
    def _():
        sout_ref[...] = state_ref[...]


def _retention(rq, rk, rv, state0, chunk, true_len):
    b = state0.shape[0]
    n = rq.shape[0] // (b * chunk)
    blk = pl.BlockSpec((chunk, RET_W), lambda i, c: (i * n + c, 0))
    st = pl.BlockSpec((None, RET_HEADS, RET_DK, RET_DV), lambda i, c: (i, 0, 0, 0))
    return pl.pallas_call(
        functools.partial(_ret_kernel, chunk=chunk, true_len=true_len),
        grid=(b, n),
        in_specs=[blk, blk, blk, st],
        out_specs=[blk, st],
        out_shape=[jax.ShapeDtypeStruct(rq.shape, F32), jax.ShapeDtypeStruct(state0.shape, F32)],
        scratch_shapes=[pltpu.VMEM((RET_HEADS, RET_DK, RET_DV), F32)],
        compiler_params=pltpu.CompilerParams(vmem_limit_bytes=VMEM_LIMIT,
                                             dimension_semantics=("arbitrary", "arbitrary")),
        name="retention",
    )(rq, rk, rv, state0)


def _mix_kernel(x_ref, g_ref, sh_ref, sc_ref, gt_ref, attn_ref, ret_ref, rg_ref, gret_ref,
                wga_ref, wgr_ref, wa_ref, wr_ref, wo_ref, o_ref):
    x = x_ref[...]
    hb = _norm_mod(x, g_ref[...], sh_ref[...], sc_ref[...]).astype(BF16)
    ret = ret_ref[...]
    rg = rg_ref[...]
    parts = []
    for hd in range(RET_HEADS):
        sl = slice(hd * RET_DV, (hd + 1) * RET_DV)
        r = ret[:, sl]
        ms = jnp.mean(r * r, axis=-1, keepdims=True)
        gate = rg[:, sl]
        parts.append(r * lax.rsqrt(ms + NORM_EPS) * gret_ref[:, sl] * (gate * jax.nn.sigmoid(gate)))
    retn = jnp.concatenate(parts, axis=-1).astype(BF16)
    y_a = _dot(attn_ref[...].astype(BF16), wa_ref[...])
    y_r = _dot(retn, wr_ref[...])
    ga = jax.nn.sigmoid(_dot(hb, wga_ref[...]))
    gr = jax.nn.sigmoid(_dot(hb, wgr_ref[...]))
    mix = _dot((ga * y_a + gr * y_r).astype(BF16), wo_ref[...])
    o_ref[...] = x + gt_ref[...] * mix


def _mix(x, g, shift, scale, gate, attn, ret, rg, gret, wga, wgr, wa, wr, wo, tm):
    n, d = x.shape
    nt = n // tm
    groups, r, _ = shift.shape
    per = nt // groups if r == 1 else nt
    row = lambda w: pl.BlockSpec((tm, w), lambda i: (i, 0))
    full = lambda a: pl.BlockSpec(a.shape, lambda i: (0,) * a.ndim)
    mod = pl.BlockSpec((None, r, d), lambda i: (i // per, 0, 0))
    return pl.pallas_call(
        _mix_kernel,
        grid=(nt,),
        in_specs=[row(d), full(g), mod, mod, mod, row(ATTN_W), row(RET_W), row(RET_W), full(gret),
                  full(wga), full(wgr), full(wa), full(wr), full(wo)],
        out_specs=row(d),
        out_shape=jax.ShapeDtypeStruct((n, d), F32),
        compiler_params=pltpu.CompilerParams(vmem_limit_bytes=VMEM_LIMIT),
        name="mix",
    )(x, g, shift, scale, gate, attn, ret, rg, gret, wga, wgr, wa, wr, wo)


_ROUTER_GROUP_LANE = N_EXPERTS


def _moe_kernel(x_ref, g_ref, sh_ref, sc_ref, gt_ref, wr_ref, br_ref, w1_ref, w3_ref, w2_ref,
                o_ref, act_ref):
    tm = x_ref.shape[0]
    x = x_ref[...]
    h = _norm_mod(x, g_ref[...], sh_ref[...], sc_ref[...])
    hb = h.astype(BF16)
    logits = _dot3(_split3(h), _split3(wr_ref[...])) + br_ref[...]
    lane_f = _lane_iota((tm, LANES)).astype(F32)

    def first_lane(hit):
        return jnp.min(jnp.where(hit, lane_f, float(LANES)), axis=-1, keepdims=True)

    is_group = jnp.abs(lane_f - (_ROUTER_GROUP_LANE + (N_GROUPS - 1) / 2)) < N_GROUPS / 2
    gl = jnp.where(is_group, logits, -jnp.inf)
    gmax = jnp.max(gl, axis=-1, keepdims=True)
    g_idx = first_lane(gl == gmax) - _ROUTER_GROUP_LANE
    g_w = 1.0 / jnp.sum(jnp.exp(gl - gmax), axis=-1, keepdims=True)
    group_mid = g_idx * EXPERTS_PER_GROUP + (EXPERTS_PER_GROUP - 1) / 2
    el = jnp.where(jnp.abs(lane_f - group_mid) < EXPERTS_PER_GROUP / 2, logits, -jnp.inf)
    m1 = jnp.max(el, axis=-1, keepdims=True)
    i1 = first_lane(el == m1)
    el2 = jnp.where(lane_f == i1, -jnp.inf, el)
    m2 = jnp.max(el2, axis=-1, keepdims=True)
    i2 = first_lane(el2 == m2)
    e2 = jnp.exp(m2 - m1)
    v1 = 1.0 / (1.0 + e2)
    v2 = e2 / (1.0 + e2)
    comb = jnp.where(lane_f == i1, g_w * v1, jnp.where(lane_f == i2, g_w * v2, 0.0))

    for e in range(N_EXPERTS):
        a = _dot(hb, w1_ref[e])
        b = _dot(hb, w3_ref[e])
        act_ref[:, e * D_EXPERT:(e + 1) * D_EXPERT] = ((a * jax.nn.sigmoid(a)) * b * comb[:, e:e + 1]).astype(BF16)
    o_ref[...] = x + gt_ref[...] * _dot(act_ref[...], w2_ref[...])


def _moe(x, g, shift, scale, gate, w_route, b_route, w1, w3, w2, tm):
    n, d = x.shape
    nt = n // tm
    groups, r, _ = shift.shape
    per = nt // groups if r == 1 else nt
    row = pl.BlockSpec((tm, d), lambda i: (i, 0))
    full = lambda a: pl.BlockSpec(a.shape, lambda i: (0,) * a.ndim)
    once = lambda a: pl.BlockSpec(a.shape, lambda i: (0,) * a.ndim, pipeline_mode=pl.Buffered(1))
    mod = pl.BlockSpec((None, r, d), lambda i: (i // per, 0, 0))
    return pl.pallas_call(
        _moe_kernel,
        grid=(nt,),
        in_specs=[row, full(g), mod, mod, mod, full(w_route), full(b_route), once(w1), once(w3), once(w2)],
        out_specs=row,
        out_shape=jax.ShapeDtypeStruct((n, d), F32),
        scratch_shapes=[pltpu.VMEM((tm, w2.shape[0]), BF16)],
        compiler_params=pltpu.CompilerParams(vmem_limit_bytes=VMEM_LIMIT),
        name="moe",
    )(x, g, shift, scale, gate, w_route, b_route, w1, w3, w2)


def _rope_tables(pos, dim):
    inv = ROPE_THETA ** (-jnp.arange(0, dim, 2, dtype=F32) / dim)
    ang = pos.astype(F32)[:, None] * inv[None, :]
    cos = jnp.concatenate([jnp.cos(ang), jnp.cos(ang)], axis=-1)
    sin = jnp.concatenate([-jnp.sin(ang), jnp.sin(ang)], axis=-1)
    rep = LANES // dim
    return jnp.tile(cos, (1, rep)), jnp.tile(sin, (1, rep))


def _pack_weights(l, w_in, g_q, g_k, g_kidx, w_group, b_group, w_router, b_router):
    offs = np.cumsum((ATTN_W, ATTN_W, ATTN_W, IDX_HEADS * IDX_DIM, IDX_DIM, IDX_HEADS,
                      RET_W, RET_W, RET_W, RET_W))
    w = w_in[l]
    d = w.shape[0]
    q, k, v, qi, ki, wi, rq, rk, rv, rg = (w[:, a:b] for a, b in zip((0, *offs[:-1]), offs))
    w_gate = w[:, offs[-1]:]
    wm = jnp.concatenate([q, k, v, rq, rk, rv, rg], axis=1).astype(BF16)
    zeros = lambda n: jnp.zeros((d, n), F32)
    widx = jnp.concatenate([qi, ki, zeros(LANES - IDX_DIM), wi, zeros(LANES - IDX_HEADS)], axis=1)
    half = w_gate.shape[1] // 2
    wga, wgr = w_gate[:, :half].astype(BF16), w_gate[:, half:].astype(BF16)
    gq = jnp.tile(g_q[l], LANES // HEAD_DIM)[None, :]
    gk = jnp.tile(g_k[l], LANES // HEAD_DIM)[None, :]
    gki = jnp.concatenate([g_kidx[l], jnp.zeros((LANES - IDX_DIM,), F32)])[None, :]
    pad = LANES - N_EXPERTS - N_GROUPS
    w_route = jnp.concatenate([w_router[l], w_group[l], zeros(pad)], axis=1)
    b_route = jnp.concatenate([b_router[l], b_group[l], jnp.zeros((pad,), F32)])[None, :]
    return wm, widx, wga, wgr, gq, gk, gki, w_route, b_route


def kernel(x_prompt, x_sample, cache_k, cache_v, cache_kidx, state_ret, page_table, c_prompt, c_sample,
           w_ada, b_ada, g_mix, g_ffn, w_in, g_q, g_k, g_kidx, g_ret, w_attn_out, w_ret_out, w_o,
           w_group, b_group, w_router, b_router, w_e1, w_e3, w_e2):
    bp, seq, d = x_prompt.shape
    bd, tokens, _ = x_sample.shape
    depth = w_in.shape[0]
    n_pages = page_table.shape[1]
    past = n_pages * PAGE_SIZE
    n_p, n_s = bp * seq, bd * tokens
    assert tokens <= _ROWS and n_pages % _IDX_PAGES == 0 and seq % RET_CHUNK == 0

    tabs_p = _rope_tables(jnp.arange(seq), HEAD_DIM) + _rope_tables(jnp.arange(seq), RET_DK)
    tabs_p = tuple(jnp.tile(t, (bp, 1)) for t in tabs_p)
    pos_s = past + jnp.arange(tokens)
    tabs_s = _rope_tables(pos_s, HEAD_DIM) + _rope_tables(pos_s, RET_DK)
    tabs_s = tuple(jnp.tile(t, (bd, 1)) for t in tabs_s)

    c_all = jnp.concatenate([c_prompt, c_sample], axis=0)
    c_rows = -(-c_all.shape[0] // 8) * 8
    c_all = jnp.pad(c_all, ((0, c_rows - c_all.shape[0]), (0, 0)))
    kidx_t = cache_kidx.transpose(0, 1, 3, 2)
    k_t = cache_k.transpose(0, 1, 3, 4, 2).reshape(cache_k.shape[:2] + (ATTN_W, PAGE_SIZE))
    v_t = cache_v.transpose(0, 1, 3, 4, 2).reshape(cache_v.shape[:2] + (ATTN_W, PAGE_SIZE))
    ret_pad = 16
    tm_p = 256 if seq % 256 == 0 else 128
    tm_moe = 512 if seq % 512 == 0 else tm_p

    y_p = x_prompt.reshape(n_p, d)
    y_s = x_sample.reshape(n_s, d)
    outs = {name: [] for name in ("kp", "vp", "kip", "sp", "ks", "vs", "kis", "ss")}
    for l in range(depth):
        wm, widx, wga, wgr, gq, gk, gki, w_route, b_route = _pack_weights(
            l, w_in, g_q, g_k, g_kidx, w_group, b_group, w_router, b_router)
        wa, wr, wo = w_attn_out[l].astype(BF16), w_ret_out[l].astype(BF16), w_o[l].astype(BF16)
        w1, w3 = w_e1[l].astype(BF16), w_e3[l].astype(BF16)
        w2 = w_e2[l].astype(BF16).reshape(N_EXPERTS * D_EXPERT, d)
        gmix, gffn, gret = g_mix[l][None, :], g_ffn[l][None, :], g_ret[l][None, :]

        ada = _ada(c_all, w_ada[l], b_ada[l])
        mods_p = [m[:bp, None, :] for m in jnp.split(ada, 6, axis=-1)]
        mods_s = [jnp.repeat(m[bp:bp + bd], tokens, axis=0)[None] for m in jnp.split(ada, 6, axis=-1)]

        sh_m, sc_m, gt_m, sh_f, sc_f, gt_f = mods_p
        (qb, k, kb, v, qi, ki, kcat, wi, rq, rk, rv, rg, qbt, kt, vt, vbt, qcatt, wt) = _proj(
            y_p, gmix, sh_m, sc_m, tabs_p, wm, widx, gq, gk, gki, tm_p)
        attn = _prompt_attend(qbt, qcatt, wt, kcat, kb, vbt, bp, seq).transpose(0, 2, 1).reshape(n_p, ATTN_W)
        ret, s_fin = _retention(rq, rk, rv, jnp.zeros((bp, RET_HEADS, RET_DK, RET_DV), F32),
                                RET_CHUNK, RET_CHUNK)
        y_p = _mix(y_p, gmix, sh_m, sc_m, gt_m, attn, ret, rg, gret, wga, wgr, wa, wr, wo, tm_p)
        y_p = _moe(y_p, gffn, sh_f, sc_f, gt_f, w_route, b_route, w1, w3, w2, tm_moe)
        by_token = lambda a: a.reshape(bp, N_HEADS, HEAD_DIM, seq).transpose(0, 3, 1, 2)
        outs["kp"].append(by_token(kt))
        outs["vp"].append(by_token(vt))
        outs["kip"].append(ki.reshape(bp, seq, IDX_DIM))
        outs["sp"].append(s_fin)

        sh_m, sc_m, gt_m, sh_f, sc_f, gt_f = mods_s
        (qb, k, kb, v, qi, ki, kcat, wi, rq, rk, rv, rg) = _proj(
            y_s, gmix, sh_m, sc_m, tabs_s, wm, widx, gq, gk, gki, n_s)[:12]
        pad_to = lambda a, rows: jnp.pad(a, ((0, 0), (0, rows - a.shape[1])) + ((0, 0),) * (a.ndim - 2))
        q32 = pad_to(qi.reshape(bd, tokens, IDX_HEADS, IDX_DIM), _ROWS).transpose(0, 2, 1, 3)
        q32 = q32.reshape(bd, IDX_HEADS * _ROWS, IDX_DIM)
        w8 = pad_to(wi.reshape(bd, tokens, LANES), _ROWS)
        new_t = lambda a: pad_to(a.reshape(bd, tokens, -1), LANES).transpose(0, 2, 1)
        plane = _sample_select(page_table, l, kidx_t, q32, w8, new_t(ki), tokens)
        bias = pad_to(plane[:, :, :n_pages + 1].reshape(bd, tokens, past + LANES), _ROWS)
        qh = pad_to(qb.astype(F32).reshape(bd, tokens, N_HEADS, HEAD_DIM), _ROWS).transpose(0, 2, 1, 3)
        q_bd = jnp.einsum('bhtd,hg->bhtgd', qh, jnp.eye(N_HEADS, dtype=F32)).reshape(bd, N_HEADS * _ROWS, ATTN_W)
        attn = _sample_attend(page_table, l, k_t, v_t, q_bd, bias, new_t(k), new_t(v))
        attn = attn[:, :tokens].reshape(n_s, ATTN_W)
        pad_chunk = lambda a: jnp.pad(a.reshape(bd, tokens, RET_W),
                                      ((0, 0), (0, ret_pad - tokens), (0, 0))).reshape(bd * ret_pad, RET_W)
        ret, s_fin = _retention(pad_chunk(rq), pad_chunk(rk), pad_chunk(rv), state_ret[l], ret_pad, tokens)
        ret = ret.reshape(bd, ret_pad, RET_W)[:, :tokens].reshape(n_s, RET_W)
        y_s = _mix(y_s, gmix, sh_m, sc_m, gt_m, attn, ret, rg, gret, wga, wgr, wa, wr, wo, n_s)
        y_s = _moe(y_s, gffn, sh_f, sc_f, gt_f, w_route, b_route, w1, w3, w2, n_s)
        outs["ks"].append(k.reshape(bd, tokens, N_HEADS, HEAD_DIM))
        outs["vs"].append(v.reshape(bd, tokens, N_HEADS, HEAD_DIM))
        outs["kis"].append(ki.reshape(bd, tokens, IDX_DIM))
        outs["ss"].append(s_fin)

    st = lambda name: jnp.stack(outs[name])
    return (y_p.reshape(bp, seq, d), y_s.reshape(bd, tokens, d),
            st("kp"), st("vp"), st("kip"), st("sp"), st("ks"), st("vs"), st("kis"), st("ss"))
```

```python
import functools
import math

import numpy as np
import jax
import jax.numpy as jnp
from jax import lax
from jax.experimental import pallas as pl
from jax.experimental.pallas import tpu as pltpu

N_HEADS = 8
HEAD_DIM = 64
IDX_HEADS = 4
IDX_DIM = 64
TOPK_MAX = 256
PAGE_SIZE = 128
RET_HEADS = 4
RET_DK = 128
RET_DV = 128
RET_CHUNK = 128
N_GROUPS = 4
EXPERTS_PER_GROUP = 4
N_EXPERTS = N_GROUPS * EXPERTS_PER_GROUP
D_EXPERT = 256
ROPE_THETA = 10000.0
NORM_EPS = 1e-6
ATTN_W = N_HEADS * HEAD_DIM
RET_W = RET_HEADS * RET_DK

LANES = 128
VMEM_LIMIT = 56 * 1024 * 1024
NEG_BIG = -1e30
_Q_SCALE = HEAD_DIM ** -0.5 * math.log2(math.e)
INT_MIN = -2 ** 31

F32 = jnp.float32
BF16 = jnp.bfloat16
I32 = jnp.int32


def _dot(a, b):
    return jnp.dot(a, b, preferred_element_type=F32)


def _dot_nt(a, b):
    return lax.dot_general(a, b, (((1,), (1,)), ((), ())), preferred_element_type=F32)


def _dot_tn(a, b):
    return lax.dot_general(a, b, (((0,), (0,)), ((), ())), preferred_element_type=F32)


def _split3(a):
    a1 = a.astype(BF16)
    r1 = a - a1.astype(F32)
    a2 = r1.astype(BF16)
    a3 = (r1 - a2.astype(F32)).astype(BF16)
    return a1, a2, a3


def _dot3(a3, b3, dot=_dot, terms=6):
    a1, a2, a_3 = a3
    b1, b2, b_3 = b3
    mid = dot(a1, b2) + dot(a2, b1)
    if terms == 6:
        mid = mid + (dot(a1, b_3) + dot(a_3, b1) + dot(a2, b2))
    return dot(a1, b1) + mid


def _norm_mod(x, g, shift, scale):
    ms = jnp.mean(x * x, axis=-1, keepdims=True)
    y = x * lax.rsqrt(ms + NORM_EPS) * g
    return y * (1.0 + scale) + shift


def _lane_iota(shape):
    return lax.broadcasted_iota(I32, shape, len(shape) - 1)


def _rope(x, cos, sin, half):
    if 2 * half == LANES:
        swapped = pltpu.roll(x, half, 1)
    else:
        first = (_lane_iota(x.shape) & (2 * half - 1)) < half
        swapped = jnp.where(first, pltpu.roll(x, LANES - half, 1), pltpu.roll(x, half, 1))
    return x * cos + swapped * sin


def _group_ones(width):
    r = lax.broadcasted_iota(I32, (LANES, LANES), 0) // width
    c = lax.broadcasted_iota(I32, (LANES, LANES), 1) // width
    return jnp.where(r == c, 1.0, 0.0).astype(BF16)


def _group_mean_sq(x, ones, width):
    y1, y2, y3 = _split3(x * x)
    return (_dot(y1, ones) + (_dot(y2, ones) + _dot(y3, ones))) * (1.0 / width)


def _sortable(x):
    bits = lax.bitcast_convert_type(x, I32)
    return bits ^ ((bits >> 31) & jnp.int32(0x7FFFFFFF))


def _ada_kernel(c_ref, w_ref, b_ref, o_ref):
    c = c_ref[...]
    s = c * jax.nn.sigmoid(c)
    o_ref[...] = _dot3(_split3(s), _split3(w_ref[...])) + b_ref[...]


def _ada(c_all, w_ada, b_ada):
    m, d = c_all.shape
    n = w_ada.shape[1]
    tn = 1024
    return pl.pallas_call(
        _ada_kernel,
        grid=(n // tn,),
        in_specs=[pl.BlockSpec((m, d), lambda j: (0, 0)),
                  pl.BlockSpec((d, tn), lambda j: (0, j)),
                  pl.BlockSpec((1, tn), lambda j: (0, j))],
        out_specs=pl.BlockSpec((m, tn), lambda j: (0, j)),
        out_shape=jax.ShapeDtypeStruct((m, n), F32),
        compiler_params=pltpu.CompilerParams(vmem_limit_bytes=VMEM_LIMIT),
        name="ada",
    )(c_all, w_ada, b_ada.reshape(1, n))


_OFF_Q, _OFF_K, _OFF_V, _OFF_RQ, _OFF_RK, _OFF_RV, _OFF_RG = (0, 512, 1024, 1536, 2048, 2560, 3072)
_W_MAIN = 3584
_W_IDX = 512


def _split_concat(x, hi_half):
    x1, x2, _ = _split3(x)
    x1, x2 = x1.astype(F32), x2.astype(F32)
    lo = _lane_iota(x.shape) < LANES // 2
    r1, r2 = pltpu.roll(x1, LANES // 2, 1), pltpu.roll(x2, LANES // 2, 1)
    if hi_half:
        return jnp.where(lo, r1, x1), jnp.where(lo, r2, 0.0)
    return jnp.where(lo, x1, r1), jnp.where(lo, x2, 0.0)


_TQ = 128


def _proj_kernel(x_ref, g_ref, sh_ref, sc_ref, c64_ref, s64_ref, c128_ref, s128_ref,
                 wm_ref, wi_ref, gq_ref, gk_ref, gki_ref,
                 qb_ref, k_ref, kb_ref, v_ref, qi_ref, ki_ref, kcat_ref, wo_ref, rq_ref, rk_ref, rv_ref, rg_ref,
                 qbt_ref, kt_ref, vt_ref, vbt_ref, qcatt_ref, wt_ref):
    tm = x_ref.shape[0]
    h = _norm_mod(x_ref[...], g_ref[...], sh_ref[...], sc_ref[...])
    hb = h.astype(BF16)
    c64, s64 = c64_ref[...], s64_ref[...]
    c128, s128 = c128_ref[...], s128_ref[...]
    ones64 = _group_ones(HEAD_DIM)
    gq, gk = gq_ref[...], gk_ref[...]

    def seg(off, width):
        return _dot(hb, wm_ref[:, off:off + width])

    zq, zk, zv = seg(_OFF_Q, ATTN_W), seg(_OFF_K, ATTN_W), seg(_OFF_V, ATTN_W)
    for j in range(ATTN_W // LANES):
        sl = slice(j * LANES, (j + 1) * LANES)
        q = zq[:, sl]
        q = q * lax.rsqrt(_group_mean_sq(q, ones64, HEAD_DIM) + NORM_EPS) * gq
        q = _rope(q, c64, s64, HEAD_DIM // 2) * _Q_SCALE
        qb_ref[:, sl] = q.astype(BF16)
        qbt_ref[sl, :] = q.T.astype(BF16)
        k = zk[:, sl]
        k = k * lax.rsqrt(_group_mean_sq(k, ones64, HEAD_DIM) + NORM_EPS) * gk
        k = _rope(k, c64, s64, HEAD_DIM // 2)
        k_ref[:, sl] = k
        kb_ref[:, sl] = k.astype(BF16)
        kt_ref[sl, :] = k.T
        v = zv[:, sl]
        v_ref[:, sl] = v
        vt = v.T
        vt_ref[sl, :] = vt
        vbt_ref[sl, :] = vt.astype(BF16)
    zrq, zrk = seg(_OFF_RQ, RET_W), seg(_OFF_RK, RET_W)
    rv_ref[...] = seg(_OFF_RV, RET_W).astype(BF16)
    rg_ref[...] = seg(_OFF_RG, RET_W)
    for j in range(RET_W // LANES):
        sl = slice(j * LANES, (j + 1) * LANES)
        rq_ref[:, sl] = _rope(zrq[:, sl], c128, s128, RET_DK // 2).astype(BF16)
        rk_ref[:, sl] = (_rope(zrk[:, sl], c128, s128, RET_DK // 2) * (RET_DK ** -0.5)).astype(BF16)

    zi = _dot3(_split3(h), _split3(wi_ref[...]), terms=3)
    for j in range(2):
        sl = slice(j * LANES, (j + 1) * LANES)
        qi = _rope(zi[:, sl], c64, s64, IDX_DIM // 2)
        qi_ref[:, sl] = qi
        for hi_half in (False, True):
            hd = 2 * j + hi_half
            for part, rows in zip(_split_concat(qi * (IDX_DIM ** -0.5), hi_half), (slice(0, LANES), slice(LANES, 2 * LANES))):
                part_t = part.T.astype(BF16)
                for blk in range(tm // _TQ):
                    col = (blk * IDX_HEADS + hd) * _TQ
                    qcatt_ref[rows, col:col + _TQ] = part_t[:, blk * _TQ:(blk + 1) * _TQ]
    zk = zi[:, 2 * LANES: 3 * LANES]
    ms = jnp.sum(zk * zk, axis=-1, keepdims=True) * (1.0 / IDX_DIM)
    ki = _rope(zk * lax.rsqrt(ms + NORM_EPS) * gki_ref[...], c64, s64, IDX_DIM // 2)
    ki_ref[...] = ki[:, :IDX_DIM]
    k1, k2, _ = _split3(ki)
    k1, k2 = k1.astype(F32), k2.astype(F32)
    kcat_ref[:, :LANES] = (k1 + pltpu.roll(k2, LANES // 2, 1)).astype(BF16)
    kcat_ref[:, LANES:] = k1.astype(BF16)
    wo = zi[:, 3 * LANES: 4 * LANES]
    wo_ref[...] = wo
    wt_ref[...] = wo.T[:_ROWS, :]


def _proj(x, g, shift, scale, tabs, wm, wi, gq, gk, gki, tm):
    n, d = x.shape
    nt = n // tm
    assert tm % _TQ == 0
    groups, r, _ = shift.shape
    per = nt // groups if r == 1 else nt
    row = lambda w: pl.BlockSpec((tm, w), lambda i: (i, 0))
    nb = groups if r == 1 else 1
    col = lambda rows, mult=1: pl.BlockSpec((None, rows, tm * mult), lambda i: (i // (nt // nb), 0, i % (nt // nb)))
    full = lambda a: pl.BlockSpec(a.shape, lambda i: (0,) * a.ndim)
    mod = pl.BlockSpec((None, r, d), lambda i: (i // per, 0, 0))
    c64, s64, c128, s128 = tabs
    outs = [(ATTN_W, BF16), (ATTN_W, F32), (ATTN_W, BF16), (ATTN_W, F32),
            (IDX_HEADS * IDX_DIM, F32), (IDX_DIM, F32), (2 * LANES, BF16), (LANES, F32),
            (RET_W, BF16), (RET_W, BF16), (RET_W, BF16), (RET_W, F32)]
    outs_t = [(ATTN_W, 1, BF16), (ATTN_W, 1, F32), (ATTN_W, 1, F32), (ATTN_W, 1, BF16),
              (2 * LANES, IDX_HEADS, BF16), (_ROWS, 1, F32)]
    return pl.pallas_call(
        _proj_kernel,
        grid=(nt,),
        in_specs=[row(d), full(g), mod, mod, row(LANES), row(LANES), row(LANES), row(LANES),
                  full(wm), full(wi), full(gq), full(gk), full(gki)],
        out_specs=[row(w) for w, _ in outs] + [col(rows, mult) for rows, mult, _ in outs_t],
        out_shape=[jax.ShapeDtypeStruct((n, w), dt) for w, dt in outs]
        + [jax.ShapeDtypeStruct((nb, rows, n // nb * mult), dt) for rows, mult, dt in outs_t],
        compiler_params=pltpu.CompilerParams(vmem_limit_bytes=VMEM_LIMIT),
        name="proj",
    )(x, g, shift, scale, c64, s64, c128, s128, wm, wi, gq, gk, gki)


def _select_bias(s_ref, b_ref, n_chunks, kc, topk, query_pos):
    nq = s_ref.shape[1]
    fold = 64

    def partial_sums(x):
        parts = [x[j * fold:(j + 1) * fold] for j in range(kc // fold)]
        while len(parts) > 1:
            parts = [a + b for a, b in zip(parts[::2], parts[1::2])] + parts[len(parts) & ~1:]
        return parts[0]

    def column_sum(x):
        return jnp.sum(partial_sums(x), axis=0, keepdims=True)

    def count(pred):
        def body(c, cnt):
            off = pl.multiple_of(c * kc, kc)
            return cnt + partial_sums(jnp.where(pred(s_ref[pl.ds(off, kc), :]), 1.0, 0.0))
        cnt = lax.fori_loop(0, n_chunks, body, jnp.zeros((fold, nq), F32))
        return jnp.sum(cnt, axis=0, keepdims=True)

    def value_bit(it, p):
        cand = p | lax.shift_left(jnp.int32(1), 31 - it)
        t = cand ^ jnp.int32(INT_MIN)
        c = count(lambda s: s >= t)
        return jnp.where(c >= topk, cand, p)

    p = lax.fori_loop(0, 32, value_bit, jnp.zeros((1, nq), I32))
    t_star = p ^ jnp.int32(INT_MIN)
    need = topk - count(lambda s: s > t_star)

    upto = jnp.where(lax.broadcasted_iota(I32, (kc, kc), 1) <= lax.broadcasted_iota(I32, (kc, kc), 0),
                     1.0, 0.0).astype(BF16)

    def write(c, ties_before):
        off = pl.multiple_of(c * kc, kc)
        tie = jnp.where(s_ref[pl.ds(off, kc), :] == t_star, 1.0, 0.0)
        tie_b = tie.astype(BF16)
        for u in range(kc // _SUB):
            sub = pl.ds(pl.multiple_of(off + u * _SUB, _SUB), _SUB)
            s = s_ref[sub, :]
            key_pos = off + u * _SUB + lax.broadcasted_iota(I32, (_SUB, nq), 0)
            rank = ties_before + _dot(upto[u * _SUB:(u + 1) * _SUB], tie_b)
            tie_kept = jnp.where(rank <= need, s, t_star + 1) == t_star
            bias = jnp.where(s > t_star, 0.0, jnp.where(tie_kept, 0.0, NEG_BIG))
            b_ref[sub, :] = jnp.where(key_pos <= query_pos, bias, NEG_BIG)
        return ties_before + column_sum(tie)

    lax.fori_loop(0, n_chunks, write, jnp.zeros((1, nq), F32))


_SUB = 128


def _prompt_attend_kernel(qbt_ref, qcatt_ref, wt_ref, kcat_ref, kb_ref, vbt_ref, o_ref,
                          s_ref, b_ref, m_ref, l_ref, acc_ref, *, tq, kc, topk):
    i = pl.program_id(1)
    q0 = i * tq
    n_chunks = (q0 + tq + kc - 1) // kc
    query_pos = q0 + _lane_iota((1, tq))
    w = wt_ref[...] * (IDX_HEADS ** -0.5)

    def score_chunk(c, carry):
        for u in range(kc // _SUB):
            off = pl.multiple_of(c * kc + u * _SUB, _SUB)
            s = _dot(kcat_ref[pl.ds(off, _SUB), :], qcatt_ref[...])
            acc = None
            for hd in range(IDX_HEADS):
                term = jnp.maximum(s[:, hd * tq:(hd + 1) * tq], 0.0) * w[hd:hd + 1, :]
                acc = term if acc is None else acc + term
            key_pos = off + lax.broadcasted_iota(I32, (_SUB, tq), 0)
            s_ref[pl.ds(off, _SUB), :] = jnp.where(key_pos <= query_pos, _sortable(acc + 0.0),
                                                   jnp.int32(INT_MIN))
        return carry

    lax.fori_loop(0, n_chunks, score_chunk, 0)
    _select_bias(s_ref, b_ref, n_chunks, kc, topk, query_pos)

    m_ref[...] = jnp.full(m_ref.shape, NEG_BIG, F32)
    l_ref[...] = jnp.zeros(l_ref.shape, F32)
    acc_ref[...] = jnp.zeros(acc_ref.shape, F32)
    feat = lax.broadcasted_iota(I32, (LANES, tq), 0)

    def attend_chunk(c, carry):
        off = pl.multiple_of(c * kc, kc)
        for hd in range(N_HEADS):
            pair = slice(hd // 2 * LANES, (hd // 2 + 1) * LANES)
            qp = qbt_ref[pair, :]
            qh = jnp.where((feat < HEAD_DIM) == (hd % 2 == 0), qp, jnp.zeros_like(qp))
            m, l, acc = m_ref[hd], l_ref[hd], acc_ref[hd]
            for u in range(kc // _SUB):
                ks = pl.ds(pl.multiple_of(off + u * _SUB, _SUB), _SUB)
                s = _dot(kb_ref[ks, pair], qh) + b_ref[ks, :]
                m_new = jnp.maximum(m, jnp.max(s, axis=0, keepdims=True))
                a = jnp.exp2(m - m_new)
                p = jnp.exp2(s - m_new)
                l = a * l + jnp.sum(p, axis=0, keepdims=True)
                acc = a * acc + _dot(vbt_ref[hd * HEAD_DIM:(hd + 1) * HEAD_DIM, ks], p.astype(BF16))
                m = m_new
            m_ref[hd], l_ref[hd], acc_ref[hd] = m, l, acc
        return carry

    lax.fori_loop(0, n_chunks, attend_chunk, 0)
    for hd in range(N_HEADS):
        o_ref[hd * HEAD_DIM:(hd + 1) * HEAD_DIM, :] = acc_ref[hd] / l_ref[hd]


def _prompt_attend(qbt, qcatt, wt, kcat, kb, vbt, batch, seq):
    tq, kc = _TQ, min(512, seq)
    topk = min(TOPK_MAX, seq // 4)
    nq = seq // tq
    col = lambda rows, mult=1: pl.BlockSpec((None, rows, tq * mult), lambda b, i: (b, 0, i))
    res = lambda w: pl.BlockSpec((seq, w), lambda b, i: (b, 0), pipeline_mode=pl.Buffered(1))
    res_t = pl.BlockSpec((None, ATTN_W, seq), lambda b, i: (b, 0, 0), pipeline_mode=pl.Buffered(1))
    return pl.pallas_call(
        functools.partial(_prompt_attend_kernel, tq=tq, kc=kc, topk=topk),
        grid=(batch, nq),
        in_specs=[col(ATTN_W), col(qcatt.shape[1], IDX_HEADS), col(wt.shape[1]),
                  res(kcat.shape[1]), res(ATTN_W), res_t],
        out_specs=col(ATTN_W),
        out_shape=jax.ShapeDtypeStruct((batch, ATTN_W, seq), F32),
        scratch_shapes=[pltpu.VMEM((seq, tq), I32), pltpu.VMEM((seq, tq), F32),
                        pltpu.VMEM((N_HEADS, 1, tq), F32), pltpu.VMEM((N_HEADS, 1, tq), F32),
                        pltpu.VMEM((N_HEADS, HEAD_DIM, tq), F32)],
        compiler_params=pltpu.CompilerParams(vmem_limit_bytes=VMEM_LIMIT,
                                             dimension_semantics=("arbitrary", "arbitrary")),
        name="prompt_attend",
    )(qbt, qcatt, wt, kcat, kb, vbt)


_IDX_PAGES = 32
_KV_PAGES = 32
_ROWS = 8


def _sample_select_kernel(pt_ref, q_ref, w_ref, kn_ref, *rest, n_pages, topk, idx_bits, tokens):
    pages = rest[:_IDX_PAGES]
    bias_ref, s_ref = rest[_IDX_PAGES:]
    g = pl.program_id(1)
    q1, q2, _ = _split3(q_ref[...] * (IDX_DIM ** -0.5))
    q_cat = jnp.concatenate([q1, q1, q2, jnp.zeros_like(q1)], axis=1)
    w = w_ref[...] * (IDX_HEADS ** -0.5)

    def scores(keys_t):
        k1, k2, _ = _split3(keys_t)
        s = _dot(q_cat, jnp.concatenate([k1, k2, k1, jnp.zeros_like(k1)], axis=0))
        acc = None
        for hd in range(IDX_HEADS):
            term = jnp.maximum(s[hd * _ROWS:(hd + 1) * _ROWS], 0.0) * w[:, hd:hd + 1]
            acc = term if acc is None else acc + term
        return _sortable(acc + 0.0)

    sc = scores(jnp.concatenate([pages[u][...] for u in range(_IDX_PAGES)], axis=1))
    row0 = pl.multiple_of(g * _IDX_PAGES, _IDX_PAGES)
    for t in range(tokens):
        s_ref[t, pl.ds(row0, _IDX_PAGES), :] = jnp.concatenate(
            [sc[t:t + 1, u * PAGE_SIZE:(u + 1) * PAGE_SIZE] for u in range(_IDX_PAGES)], axis=0)

    @pl.when(g == pl.num_programs(1) - 1)
    def _():
        shape = (tokens, n_pages + _ROWS, LANES)
        lane1 = _lane_iota((1, LANES))
        sn = scores(kn_ref[...])
        floor = jnp.full((_ROWS - 1, LANES), INT_MIN, I32)
        for t in range(tokens):
            new_row = jnp.where(lane1 <= t, sn[t:t + 1], jnp.int32(INT_MIN))
            s_ref[t, n_pages:, :] = jnp.concatenate([new_row, floor], axis=0)

        idx = lax.broadcasted_iota(I32, shape, 1) * LANES + _lane_iota(shape)

        def count(hit):
            c = jnp.sum(jnp.where(hit, 1.0, 0.0), axis=1, keepdims=True)
            return jnp.sum(c, axis=2, keepdims=True)

        def value_bit(it, p):
            cand = p | lax.shift_left(jnp.int32(1), 31 - it)
            c = count(s_ref[...] >= (cand ^ jnp.int32(INT_MIN)))
            return jnp.where(c >= topk, cand, p)

        p = lax.fori_loop(0, 32, value_bit, jnp.zeros((tokens, 1, 1), I32))
        t_star = p ^ jnp.int32(INT_MIN)
        need = topk - count(s_ref[...] > t_star)

        def index_bit(it, r):
            cand = r | lax.shift_left(jnp.int32(1), idx_bits - 1 - it)
            c = count(jnp.where(idx < cand, s_ref[...], jnp.int32(INT_MIN)) == t_star)
            return jnp.where(c < need, cand, r)

        j_star = lax.fori_loop(0, idx_bits, index_bit, jnp.zeros((tokens, 1, 1), I32))
        s = s_ref[...]
        tie = jnp.where(idx <= j_star, s, jnp.int32(INT_MIN)) == t_star
        bias = jnp.where(s > t_star, 0.0, jnp.where(tie, 0.0, NEG_BIG))
        bias_ref[...] = jnp.where(s == jnp.int32(INT_MIN), NEG_BIG, bias)


def _sample_select(page_table, layer, kidx_t, q32, w8, ki_new_t, tokens):
    bd, n_pages = page_table.shape
    past = n_pages * PAGE_SIZE
    topk = min(TOPK_MAX, (past + tokens) // 4)
    idx_bits = (past + LANES - 1).bit_length()
    steps = n_pages // _IDX_PAGES
    page = lambda u: pl.BlockSpec((None, None, IDX_DIM, PAGE_SIZE),
                                  lambda b, g, pt: (layer, pt[b, g * _IDX_PAGES + u], 0, 0))
    per_b = lambda *blk: pl.BlockSpec((None,) + blk, lambda b, g, pt: (b,) + (0,) * len(blk))
    plane = (tokens, n_pages + _ROWS, LANES)
    grid_spec = pltpu.PrefetchScalarGridSpec(
        num_scalar_prefetch=1,
        grid=(bd, steps),
        in_specs=[per_b(IDX_HEADS * _ROWS, IDX_DIM), per_b(_ROWS, LANES), per_b(IDX_DIM, LANES)]
        + [page(u) for u in range(_IDX_PAGES)],
        out_specs=per_b(*plane),
        scratch_shapes=[pltpu.VMEM(plane, I32)],
    )
    return pl.pallas_call(
        functools.partial(_sample_select_kernel, n_pages=n_pages, topk=topk, idx_bits=idx_bits, tokens=tokens),
        grid_spec=grid_spec,
        out_shape=jax.ShapeDtypeStruct((bd,) + plane, F32),
        compiler_params=pltpu.CompilerParams(vmem_limit_bytes=VMEM_LIMIT,
                                             dimension_semantics=("arbitrary", "arbitrary")),
        name="sample_select",
    )(page_table, q32, w8, ki_new_t, *([kidx_t] * _IDX_PAGES))


def _sample_attend_kernel(pt_ref, q_ref, bias_ref, kn_ref, vn_ref, *rest, past, groups):
    kpages = rest[:_KV_PAGES]
    vpages = rest[_KV_PAGES:2 * _KV_PAGES]
    o_ref, s_ref, acc_ref = rest[2 * _KV_PAGES:]
    g = pl.program_id(1)
    width = _KV_PAGES * PAGE_SIZE

    @pl.when(g < groups)
    def _():
        keys_t = jnp.concatenate([kpages[u][...] for u in range(_KV_PAGES)], axis=1)
        s_ref[:, pl.ds(pl.multiple_of(g * width, width), width)] = _dot(q_ref[...], keys_t)

    @pl.when(g == groups - 1)
    def _():
        s_ref[:, past:] = _dot(q_ref[...], kn_ref[...])
        bias = bias_ref[...]
        for hd in range(N_HEADS):
            rows = slice(hd * _ROWS, (hd + 1) * _ROWS)
            s = s_ref[rows, :] + bias
            p = jnp.exp2(s - jnp.max(s, axis=-1, keepdims=True))
            s_ref[rows, :] = p / jnp.sum(p, axis=-1, keepdims=True)
        acc_ref[...] = jnp.zeros(acc_ref.shape, F32)

    @pl.when(g >= groups)
    def _():
        vals_t = jnp.concatenate([vpages[u][...] for u in range(_KV_PAGES)], axis=1)
        p = s_ref[:, pl.ds(pl.multiple_of((g - groups) * width, width), width)]
        acc_ref[...] += _dot_nt(p, vals_t)

    @pl.when(g == 2 * groups - 1)
    def _():
        acc = acc_ref[...] + _dot_nt(s_ref[:, past:], vn_ref[...])
        lane_head = _lane_iota((_ROWS, ATTN_W)) // HEAD_DIM
        out = jnp.zeros((_ROWS, ATTN_W), F32)
        for hd in range(N_HEADS):
            out = out + jnp.where(lane_head == hd, acc[hd * _ROWS:(hd + 1) * _ROWS], 0.0)
        o_ref[...] = out


def _sample_attend(page_table, layer, k_t, v_t, q_bd, bias, k_new_t, v_new_t):
    bd, n_pages = page_table.shape
    past = n_pages * PAGE_SIZE
    groups = n_pages // _KV_PAGES
    kpage = lambda u: pl.BlockSpec(
        (None, None, ATTN_W, PAGE_SIZE),
        lambda b, g, pt: (layer, pt[b, jnp.minimum(g, groups - 1) * _KV_PAGES + u], 0, 0))
    vpage = lambda u: pl.BlockSpec(
        (None, None, ATTN_W, PAGE_SIZE),
        lambda b, g, pt: (layer, pt[b, jnp.maximum(g - groups, 0) * _KV_PAGES + u], 0, 0))
    per_b = lambda r, w: pl.BlockSpec((None, r, w), lambda b, g, pt: (b, 0, 0))
    rows = N_HEADS * _ROWS
    grid_spec = pltpu.PrefetchScalarGridSpec(
        num_scalar_prefetch=1,
        grid=(bd, 2 * groups),
        in_specs=[per_b(rows, ATTN_W), per_b(_ROWS, past + LANES), per_b(ATTN_W, LANES), per_b(ATTN_W, LANES)]
        + [kpage(u) for u in range(_KV_PAGES)] + [vpage(u) for u in range(_KV_PAGES)],
        out_specs=per_b(_ROWS, ATTN_W),
        scratch_shapes=[pltpu.VMEM((rows, past + LANES), F32), pltpu.VMEM((rows, ATTN_W), F32)],
    )
    return pl.pallas_call(
        functools.partial(_sample_attend_kernel, past=past, groups=groups),
        grid_spec=grid_spec,
        out_shape=jax.ShapeDtypeStruct((bd, _ROWS, ATTN_W), F32),
        compiler_params=pltpu.CompilerParams(vmem_limit_bytes=VMEM_LIMIT,
                                             dimension_semantics=("arbitrary", "arbitrary")),
        name="sample_attend",
    )(page_table, q_bd, bias, k_new_t, v_new_t, *([k_t] * _KV_PAGES), *([v_t] * _KV_PAGES))


def _ret_kernel(q_ref, k_ref, v_ref, s0_ref, o_ref, sout_ref, state_ref, *, chunk, true_len):
    c = pl.program_id(1)

    @pl.when(c == 0)
    def _():
        state_ref[...] = s0_ref[...]

    ti = lax.broadcasted_iota(I32, (chunk, chunk), 0)
    tj = lax.broadcasted_iota(I32, (chunk, chunk), 1)
    diff = (ti - tj).astype(F32)
    t = lax.broadcasted_iota(I32, (chunk, 1), 0).astype(F32)
    for hd in range(RET_HEADS):
        sl = slice(hd * RET_DK, (hd + 1) * RET_DK)
        log_g = math.log(1.0 - 2.0 ** (-5.0 - hd))
        decay = jnp.where(diff >= 0, jnp.exp(log_g * jnp.maximum(diff, 0.0)), 0.0)
        q_dec = jnp.exp(log_g * (t + 1.0))
        k_dec = jnp.exp(log_g * (true_len - 1.0 - t))
        c_dec = math.exp(log_g * true_len)
        q, k, v = q_ref[:, sl], k_ref[:, sl], v_ref[:, sl]
        s = state_ref[hd]
        att = _dot_nt(q, k) * decay
        o_ref[:, sl] = _dot(att.astype(BF16), v) + _dot(q, s.astype(BF16)) * q_dec
        kd = (k.astype(F32) * k_dec).astype(BF16)
        state_ref[hd] = s * c_dec + _dot_tn(kd, v)

    @pl.when(c == pl.num_programs(1) - 1)
    def _():
        sout_ref[...] = state_ref[...]


def _retention(rq, rk, rv, state0, chunk, true_len):
    b = state0.shape[0]
    n = rq.shape[0] // (b * chunk)
    blk = pl.BlockSpec((chunk, RET_W), lambda i, c: (i * n + c, 0))
    st = pl.BlockSpec((None, RET_HEADS, RET_DK, RET_DV), lambda i, c: (i, 0, 0, 0))
    return pl.pallas_call(
        functools.partial(_ret_kernel, chunk=chunk, true_len=true_len),
        grid=(b, n),
        in_specs=[blk, blk, blk, st],
        out_specs=[blk, st],
        out_shape=[jax.ShapeDtypeStruct(rq.shape, F32), jax.ShapeDtypeStruct(state0.shape, F32)],
        scratch_shapes=[pltpu.VMEM((RET_HEADS, RET_DK, RET_DV), F32)],
        compiler_params=pltpu.CompilerParams(vmem_limit_bytes=VMEM_LIMIT,
                                             dimension_semantics=("arbitrary", "arbitrary")),
        name="retention",
    )(rq, rk, rv, state0)


def _mix_kernel(x_ref, g_ref, sh_ref, sc_ref, gt_ref, attn_ref, ret_ref, rg_ref, gret_ref,
                wga_ref, wgr_ref, wa_ref, wr_ref, wo_ref, o_ref):
    x = x_ref[...]
    hb = _norm_mod(x, g_ref[...], sh_ref[...], sc_ref[...]).astype(BF16)
    ret = ret_ref[...]
    rg = rg_ref[...]
    parts = []
    for hd in range(RET_HEADS):
        sl = slice(hd * RET_DV, (hd + 1) * RET_DV)
        r = ret[:, sl]
        ms = jnp.mean(r * r, axis=-1, keepdims=True)
        gate = rg[:, sl]
        parts.append(r * lax.rsqrt(ms + NORM_EPS) * gret_ref[:, sl] * (gate * jax.nn.sigmoid(gate)))
    retn = jnp.concatenate(parts, axis=-1).astype(BF16)
    y_a = _dot(attn_ref[...].astype(BF16), wa_ref[...])
    y_r = _dot(retn, wr_ref[...])
    ga = jax.nn.sigmoid(_dot(hb, wga_ref[...]))
    gr = jax.nn.sigmoid(_dot(hb, wgr_ref[...]))
    mix = _dot((ga * y_a + gr * y_r).astype(BF16), wo_ref[...])
    o_ref[...] = x + gt_ref[...] * mix


def _mix(x, g, shift, scale, gate, attn, ret, rg, gret, wga, wgr, wa, wr, wo, tm):
    n, d = x.shape
    nt = n // tm
    groups, r, _ = shift.shape
    per = nt // groups if r == 1 else nt
    row = lambda w: pl.BlockSpec((tm, w), lambda i: (i, 0))
    full = lambda a: pl.BlockSpec(a.shape, lambda i: (0,) * a.ndim)
    mod = pl.BlockSpec((None, r, d), lambda i: (i // per, 0, 0))
    return pl.pallas_call(
        _mix_kernel,
        grid=(nt,),
        in_specs=[row(d), full(g), mod, mod, mod, row(ATTN_W), row(RET_W), row(RET_W), full(gret),
                  full(wga), full(wgr), full(wa), full(wr), full(wo)],
        out_specs=row(d),
        out_shape=jax.ShapeDtypeStruct((n, d), F32),
        compiler_params=pltpu.CompilerParams(vmem_limit_bytes=VMEM_LIMIT),
        name="mix",
    )(x, g, shift, scale, gate, attn, ret, rg, gret, wga, wgr, wa, wr, wo)


_ROUTER_GROUP_LANE = N_EXPERTS


def _moe_kernel(x_ref, g_ref, sh_ref, sc_ref, gt_ref, wr_ref, br_ref, w1_ref, w3_ref, w2_ref,
                o_ref, act_ref):
    tm = x_ref.shape[0]
    x = x_ref[...]
    h = _norm_mod(x, g_ref[...], sh_ref[...], sc_ref[...])
    hb = h.astype(BF16)
    logits = _dot3(_split3(h), _split3(wr_ref[...])) + br_ref[...]
    lane_f = _lane_iota((tm, LANES)).astype(F32)

    def first_lane(hit):
        return jnp.min(jnp.where(hit, lane_f, float(LANES)), axis=-1, keepdims=True)

    is_group = jnp.abs(lane_f - (_ROUTER_GROUP_LANE + (N_GROUPS - 1) / 2)) < N_GROUPS / 2
    gl = jnp.where(is_group, logits, -jnp.inf)
    gmax = jnp.max(gl, axis=-1, keepdims=True)
    g_idx = first_lane(gl == gmax) - _ROUTER_GROUP_LANE
    g_w = 1.0 / jnp.sum(jnp.exp(gl - gmax), axis=-1, keepdims=True)
    group_mid = g_idx * EXPERTS_PER_GROUP + (EXPERTS_PER_GROUP - 1) / 2
    el = jnp.where(jnp.abs(lane_f - group_mid) < EXPERTS_PER_GROUP / 2, logits, -jnp.inf)
    m1 = jnp.max(el, axis=-1, keepdims=True)
    i1 = first_lane(el == m1)
    el2 = jnp.where(lane_f == i1, -jnp.inf, el)
    m2 = jnp.max(el2, axis=-1, keepdims=True)
    i2 = first_lane(el2 == m2)
    e2 = jnp.exp(m2 - m1)
    v1 = 1.0 / (1.0 + e2)
    v2 = e2 / (1.0 + e2)
    comb = jnp.where(lane_f == i1, g_w * v1, jnp.where(lane_f == i2, g_w * v2, 0.0))

    for e in range(N_EXPERTS):
        a = _dot(hb, w1_ref[e])
        b = _dot(hb, w3_ref[e])
        act_ref[:, e * D_EXPERT:(e + 1) * D_EXPERT] = ((a * jax.nn.sigmoid(a)) * b * comb[:, e:e + 1]).astype(BF16)
    o_ref[...] = x + gt_ref[...] * _dot(act_ref[...], w2_ref[...])


def _moe(x, g, shift, scale, gate, w_route, b_route, w1, w3, w2, tm):
    n, d = x.shape
    nt = n // tm
    groups, r, _ = shift.shape
    per = nt // groups if r == 1 else nt
    row = pl.BlockSpec((tm, d), lambda i: (i, 0))
    full = lambda a: pl.BlockSpec(a.shape, lambda i: (0,) * a.ndim)
    once = lambda a: pl.BlockSpec(a.shape, lambda i: (0,) * a.ndim, pipeline_mode=pl.Buffered(1))
    mod = pl.BlockSpec((None, r, d), lambda i: (i // per, 0, 0))
    return pl.pallas_call(
        _moe_kernel,
        grid=(nt,),
        in_specs=[row, full(g), mod, mod, mod, full(w_route), full(b_route), once(w1), once(w3), once(w2)],
        out_specs=row,
        out_shape=jax.ShapeDtypeStruct((n, d), F32),
        scratch_shapes=[pltpu.VMEM((tm, w2.shape[0]), BF16)],
        compiler_params=pltpu.CompilerParams(vmem_limit_bytes=VMEM_LIMIT),
        name="moe",
    )(x, g, shift, scale, gate, w_route, b_route, w1, w3, w2)


def _rope_tables(pos, dim):
    inv = ROPE_THETA ** (-jnp.arange(0, dim, 2, dtype=F32) / dim)
    ang = pos.astype(F32)[:, None] * inv[None, :]
    cos = jnp.concatenate([jnp.cos(ang), jnp.cos(ang)], axis=-1)
    sin = jnp.concatenate([-jnp.sin(ang), jnp.sin(ang)], axis=-1)
    rep = LANES // dim
    return jnp.tile(cos, (1, rep)), jnp.tile(sin, (1, rep))


def _pack_weights(l, w_in, g_q, g_k, g_kidx, w_group, b_group, w_router, b_router):
    offs = np.cumsum((ATTN_W, ATTN_W, ATTN_W, IDX_HEADS * IDX_DIM, IDX_DIM, IDX_HEADS,
                      RET_W, RET_W, RET_W, RET_W))
    w = w_in[l]
    d = w.shape[0]
    q, k, v, qi, ki, wi, rq, rk, rv, rg = (w[:, a:b] for a, b in zip((0, *offs[:-1]), offs))
    w_gate = w[:, offs[-1]:]
    wm = jnp.concatenate([q, k, v, rq, rk, rv, rg], axis=1).astype(BF16)
    zeros = lambda n: jnp.zeros((d, n), F32)
    widx = jnp.concatenate([qi, ki, zeros(LANES - IDX_DIM), wi, zeros(LANES - IDX_HEADS)], axis=1)
    half = w_gate.shape[1] // 2
    wga, wgr = w_gate[:, :half].astype(BF16), w_gate[:, half:].astype(BF16)
    gq = jnp.tile(g_q[l], LANES // HEAD_DIM)[None, :]
    gk = jnp.tile(g_k[l], LANES // HEAD_DIM)[None, :]
    gki = jnp.concatenate([g_kidx[l], jnp.zeros((LANES - IDX_DIM,), F32)])[None, :]
    pad = LANES - N_EXPERTS - N_GROUPS
    w_route = jnp.concatenate([w_router[l], w_group[l], zeros(pad)], axis=1)
    b_route = jnp.concatenate([b_router[l], b_group[l], jnp.zeros((pad,), F32)])[None, :]
    return wm, widx, wga, wgr, gq, gk, gki, w_route, b_route


def kernel(x_prompt, x_sample, cache_k, cache_v, cache_kidx, state_ret, page_table, c_prompt, c_sample,
           w_ada, b_ada, g_mix, g_ffn, w_in, g_q, g_k, g_kidx, g_ret, w_attn_out, w_ret_out, w_o,
           w_group, b_group, w_router, b_router, w_e1, w_e3, w_e2):
    bp, seq, d = x_prompt.shape
    bd, tokens, _ = x_sample.shape
    depth = w_in.shape[0]
    n_pages = page_table.shape[1]
    past = n_pages * PAGE_SIZE
    n_p, n_s = bp * seq, bd * tokens
    assert tokens <= _ROWS and n_pages % _IDX_PAGES == 0 and seq % RET_CHUNK == 0

    tabs_p = _rope_tables(jnp.arange(seq), HEAD_DIM) + _rope_tables(jnp.arange(seq), RET_DK)
    tabs_p = tuple(jnp.tile(t, (bp, 1)) for t in tabs_p)
    pos_s = past + jnp.arange(tokens)
    tabs_s = _rope_tables(pos_s, HEAD_DIM) + _rope_tables(pos_s, RET_DK)
    tabs_s = tuple(jnp.tile(t, (bd, 1)) for t in tabs_s)

    c_all = jnp.concatenate([c_prompt, c_sample], axis=0)
    c_rows = -(-c_all.shape[0] // 8) * 8
    c_all = jnp.pad(c_all, ((0, c_rows - c_all.shape[0]), (0, 0)))
    kidx_t = cache_kidx.transpose(0, 1, 3, 2)
    k_t = cache_k.transpose(0, 1, 3, 4, 2).reshape(cache_k.shape[:2] + (ATTN_W, PAGE_SIZE))
    v_t = cache_v.transpose(0, 1, 3, 4, 2).reshape(cache_v.shape[:2] + (ATTN_W, PAGE_SIZE))
    ret_pad = 16
    tm_p = 256 if seq % 256 == 0 else 128
    tm_moe = 512 if seq % 512 == 0 else tm_p

    y_p = x_prompt.reshape(n_p, d)
    y_s = x_sample.reshape(n_s, d)
    outs = {name: [] for name in ("kp", "vp", "kip", "sp", "ks", "vs", "kis", "ss")}
    for l in range(depth):
        wm, widx, wga, wgr, gq, gk, gki, w_route, b_route = _pack_weights(
            l, w_in, g_q, g_k, g_kidx, w_group, b_group, w_router, b_router)
        wa, wr, wo = w_attn_out[l].astype(BF16), w_ret_out[l].astype(BF16), w_o[l].astype(BF16)
        w1, w3 = w_e1[l].astype(BF16), w_e3[l].astype(BF16)
        w2 = w_e2[l].astype(BF16).reshape(N_EXPERTS * D_EXPERT, d)
        gmix, gffn, gret = g_mix[l][None, :], g_ffn[l][None, :], g_ret[l][None, :]

        ada = _ada(c_all, w_ada[l], b_ada[l])
        mods_p = [m[:bp, None, :] for m in jnp.split(ada, 6, axis=-1)]
        mods_s = [jnp.repeat(m[bp:bp + bd], tokens, axis=0)[None] for m in jnp.split(ada, 6, axis=-1)]

        sh_m, sc_m, gt_m, sh_f, sc_f, gt_f = mods_p
        (qb, k, kb, v, qi, ki, kcat, wi, rq, rk, rv, rg, qbt, kt, vt, vbt, qcatt, wt) = _proj(
            y_p, gmix, sh_m, sc_m, tabs_p, wm, widx, gq, gk, gki, tm_p)
        attn = _prompt_attend(qbt, qcatt, wt, kcat, kb, vbt, bp, seq).transpose(0, 2, 1).reshape(n_p, ATTN_W)
        ret, s_fin = _retention(rq, rk, rv, jnp.zeros((bp, RET_HEADS, RET_DK, RET_DV), F32),
                                RET_CHUNK, RET_CHUNK)
        y_p = _mix(y_p, gmix, sh_m, sc_m, gt_m, attn, ret, rg, gret, wga, wgr, wa, wr, wo, tm_p)
        y_p = _moe(y_p, gffn, sh_f, sc_f, gt_f, w_route, b_route, w1, w3, w2, tm_moe)
        by_token = lambda a: a.reshape(bp, N_HEADS, HEAD_DIM, seq).transpose(0, 3, 1, 2)
        outs["kp"].append(by_token(kt))
        outs["vp"].append(by_token(vt))
        outs["kip"].append(ki.reshape(bp, seq, IDX_DIM))
        outs["sp"].append(s_fin)

        sh_m, sc_m, gt_m, sh_f, sc_f, gt_f = mods_s
        (qb, k, kb, v, qi, ki, kcat, wi, rq, rk, rv, rg) = _proj(
            y_s, gmix, sh_m, sc_m, tabs_s, wm, widx, gq, gk, gki, n_s)[:12]
        pad_to = lambda a, rows: jnp.pad(a, ((0, 0), (0, rows - a.shape[1])) + ((0, 0),) * (a.ndim - 2))
        q32 = pad_to(qi.reshape(bd, tokens, IDX_HEADS, IDX_DIM), _ROWS).transpose(0, 2, 1, 3)
        q32 = q32.reshape(bd, IDX_HEADS * _ROWS, IDX_DIM)
        w8 = pad_to(wi.reshape(bd, tokens, LANES), _ROWS)
        new_t = lambda a: pad_to(a.reshape(bd, tokens, -1), LANES).transpose(0, 2, 1)
        plane = _sample_select(page_table, l, kidx_t, q32, w8, new_t(ki), tokens)
        bias = pad_to(plane[:, :, :n_pages + 1].reshape(bd, tokens, past + LANES), _ROWS)
        qh = pad_to(qb.astype(F32).reshape(bd, tokens, N_HEADS, HEAD_DIM), _ROWS).transpose(0, 2, 1, 3)
        q_bd = jnp.einsum('bhtd,hg->bhtgd', qh, jnp.eye(N_HEADS, dtype=F32)).reshape(bd, N_HEADS * _ROWS, ATTN_W)
        attn = _sample_attend(page_table, l, k_t, v_t, q_bd, bias, new_t(k), new_t(v))
        attn = attn[:, :tokens].reshape(n_s, ATTN_W)
        pad_chunk = lambda a: jnp.pad(a.reshape(bd, tokens, RET_W),
                                      ((0, 0), (0, ret_pad - tokens), (0, 0))).reshape(bd * ret_pad, RET_W)
        ret, s_fin = _retention(pad_chunk(rq), pad_chunk(rk), pad_chunk(rv), state_ret[l], ret_pad, tokens)
        ret = ret.reshape(bd, ret_pad, RET_W)[:, :tokens].reshape(n_s, RET_W)
        y_s = _mix(y_s, gmix, sh_m, sc_m, gt_m, attn, ret, rg, gret, wga, wgr, wa, wr, wo, n_s)
        y_s = _moe(y_s, gffn, sh_f, sc_f, gt_f, w_route, b_route, w1, w3, w2, n_s)
        outs["ks"].append(k.reshape(bd, tokens, N_HEADS, HEAD_DIM))
        outs["vs"].append(v.reshape(bd, tokens, N_HEADS, HEAD_DIM))
        outs["kis"].append(ki.reshape(bd, tokens, IDX_DIM))
        outs["ss"].append(s_fin)

    st = lambda name: jnp.stack(outs[name])
    return (y_p.reshape(bp, seq, d), y_s.reshape(bd, tokens, d),
            st("kp"), st("vp"), st("kip"), st("sp"), st("ks"), st("vs"), st("kis"), st("ss"))
```

```python
import functools
import math

import numpy as np
import jax
import jax.numpy as jnp
from jax import lax
from jax.experimental import pallas as pl
from jax.experimental.pallas import tpu as pltpu

N_HEADS = 8
HEAD_DIM = 64
IDX_HEADS = 4
IDX_DIM = 64
TOPK_MAX = 256
PAGE_SIZE = 128
RET_HEADS = 4
RET_DK = 128
RET_DV = 128
RET_CHUNK = 128
N_GROUPS = 4
EXPERTS_PER_GROUP = 4
N_EXPERTS = N_GROUPS * EXPERTS_PER_GROUP
D_EXPERT = 256
ROPE_THETA = 10000.0
NORM_EPS = 1e-6
ATTN_W = N_HEADS * HEAD_DIM
RET_W = RET_HEADS * RET_DK

LANES = 128
VMEM_LIMIT = 56 * 1024 * 1024
NEG_BIG = -1e30
_Q_SCALE = HEAD_DIM ** -0.5 * math.log2(math.e)
INT_MIN = -2 ** 31

F32 = jnp.float32
BF16 = jnp.bfloat16
I32 = jnp.int32


def _dot(a, b):
    return jnp.dot(a, b, preferred_element_type=F32)


def _dot_nt(a, b):
    return lax.dot_general(a, b, (((1,), (1,)), ((), ())), preferred_element_type=F32)


def _dot_tn(a, b):
    return lax.dot_general(a, b, (((0,), (0,)), ((), ())), preferred_element_type=F32)


def _split3(a):
    a1 = a.astype(BF16)
    r1 = a - a1.astype(F32)
    a2 = r1.astype(BF16)
    a3 = (r1 - a2.astype(F32)).astype(BF16)
    return a1, a2, a3


def _dot3(a3, b3, dot=_dot, terms=6):
    a1, a2, a_3 = a3
    b1, b2, b_3 = b3
    mid = dot(a1, b2) + dot(a2, b1)
    if terms == 6:
        mid = mid + (dot(a1, b_3) + dot(a_3, b1) + dot(a2, b2))
    return dot(a1, b1) + mid


def _norm_mod(x, g, shift, scale):
    ms = jnp.mean(x * x, axis=-1, keepdims=True)
    y = x * lax.rsqrt(ms + NORM_EPS) * g
    return y * (1.0 + scale) + shift


def _lane_iota(shape):
    return lax.broadcasted_iota(I32, shape, len(shape) - 1)


def _rope(x, cos, sin, half):
    if 2 * half == LANES:
        swapped = pltpu.roll(x, half, 1)
    else:
        first = (_lane_iota(x.shape) & (2 * half - 1)) < half
        swapped = jnp.where(first, pltpu.roll(x, LANES - half, 1), pltpu.roll(x, half, 1))
    return x * cos + swapped * sin


def _group_ones(width):
    r = lax.broadcasted_iota(I32, (LANES, LANES), 0) // width
    c = lax.broadcasted_iota(I32, (LANES, LANES), 1) // width
    return jnp.where(r == c, 1.0, 0.0).astype(BF16)


def _group_mean_sq(x, ones, width):
    y1, y2, y3 = _split3(x * x)
    return (_dot(y1, ones) + (_dot(y2, ones) + _dot(y3, ones))) * (1.0 / width)


def _sortable(x):
    bits = lax.bitcast_convert_type(x, I32)
    return bits ^ ((bits >> 31) & jnp.int32(0x7FFFFFFF))


def _ada_kernel(c_ref, w_ref, b_ref, o_ref):
    c = c_ref[...]
    s = c * jax.nn.sigmoid(c)
    o_ref[...] = _dot3(_split3(s), _split3(w_ref[...])) + b_ref[...]


def _ada(c_all, w_ada, b_ada):
    m, d = c_all.shape
    n = w_ada.shape[1]
    tn = 1024
    return pl.pallas_call(
        _ada_kernel,
        grid=(n // tn,),
        in_specs=[pl.BlockSpec((m, d), lambda j: (0, 0)),
                  pl.BlockSpec((d, tn), lambda j: (0, j)),
                  pl.BlockSpec((1, tn), lambda j: (0, j))],
        out_specs=pl.BlockSpec((m, tn), lambda j: (0, j)),
        out_shape=jax.ShapeDtypeStruct((m, n), F32),
        compiler_params=pltpu.CompilerParams(vmem_limit_bytes=VMEM_LIMIT),
        name="ada",
    )(c_all, w_ada, b_ada.reshape(1, n))


_OFF_Q, _OFF_K, _OFF_V, _OFF_RQ, _OFF_RK, _OFF_RV, _OFF_RG = (0, 512, 1024, 1536, 2048, 2560, 3072)
_W_MAIN = 3584
_W_IDX = 512


def _split_concat(x, hi_half):
    x1, x2, _ = _split3(x)
    x1, x2 = x1.astype(F32), x2.astype(F32)
    lo = _lane_iota(x.shape) < LANES // 2
    r1, r2 = pltpu.roll(x1, LANES // 2, 1), pltpu.roll(x2, LANES // 2, 1)
    if hi_half:
        return jnp.where(lo, r1, x1), jnp.where(lo, r2, 0.0)
    return jnp.where(lo, x1, r1), jnp.where(lo, x2, 0.0)


_TQ = 128


def _proj_kernel(x_ref, g_ref, sh_ref, sc_ref, c64_ref, s64_ref, c128_ref, s128_ref,
                 wm_ref, wi_ref, gq_ref, gk_ref, gki_ref,
                 qb_ref, k_ref, kb_ref, v_ref, qi_ref, ki_ref, kcat_ref, wo_ref, rq_ref, rk_ref, rv_ref, rg_ref,
                 qbt_ref, kt_ref, vt_ref, vbt_ref, qcatt_ref, wt_ref):
    tm = x_ref.shape[0]
    h = _norm_mod(x_ref[...], g_ref[...], sh_ref[...], sc_ref[...])
    hb = h.astype(BF16)
    c64, s64 = c64_ref[...], s64_ref[...]
    c128, s128 = c128_ref[...], s128_ref[...]
    ones64 = _group_ones(HEAD_DIM)
    gq, gk = gq_ref[...], gk_ref[...]

    def seg(off, width):
        return _dot(hb, wm_ref[:, off:off + width])

    zq, zk, zv = seg(_OFF_Q, ATTN_W), seg(_OFF_K, ATTN_W), seg(_OFF_V, ATTN_W)
    for j in range(ATTN_W // LANES):
        sl = slice(j * LANES, (j + 1) * LANES)
        q = zq[:, sl]
        q = q * lax.rsqrt(_group_mean_sq(q, ones64, HEAD_DIM) + NORM_EPS) * gq
        q = _rope(q, c64, s64, HEAD_DIM // 2) * _Q_SCALE
        qb_ref[:, sl] = q.astype(BF16)
        qbt_ref[sl, :] = q.T.astype(BF16)
        k = zk[:, sl]
        k = k * lax.rsqrt(_group_mean_sq(k, ones64, HEAD_DIM) + NORM_EPS) * gk
        k = _rope(k, c64, s64, HEAD_DIM // 2)
        k_ref[:, sl] = k
        kb_ref[:, sl] = k.astype(BF16)
        kt_ref[sl, :] = k.T
        v = zv[:, sl]
        v_ref[:, sl] = v
        vt = v.T
        vt_ref[sl, :] = vt
        vbt_ref[sl, :] = vt.astype(BF16)
    zrq, zrk = seg(_OFF_RQ, RET_W), seg(_OFF_RK, RET_W)
    rv_ref[...] = seg(_OFF_RV, RET_W).astype(BF16)
    rg_ref[...] = seg(_OFF_RG, RET_W)
    for j in range(RET_W // LANES):
        sl = slice(j * LANES, (j + 1) * LANES)
        rq_ref[:, sl] = _rope(zrq[:, sl], c128, s128, RET_DK // 2).astype(BF16)
        rk_ref[:, sl] = (_rope(zrk[:, sl], c128, s128, RET_DK // 2) * (RET_DK ** -0.5)).astype(BF16)

    zi = _dot3(_split3(h), _split3(wi_ref[...]), terms=3)
    for j in range(2):
        sl = slice(j * LANES, (j + 1) * LANES)
        qi = _rope(zi[:, sl], c64, s64, IDX_DIM // 2)
        qi_ref[:, sl] = qi
        for hi_half in (False, True):
            hd = 2 * j + hi_half
            for part, rows in zip(_split_concat(qi * (IDX_DIM ** -0.5), hi_half), (slice(0, LANES), slice(LANES, 2 * LANES))):
                part_t = part.T.astype(BF16)
                for blk in range(tm // _TQ):
                    col = (blk * IDX_HEADS + hd) * _TQ
                    qcatt_ref[rows, col:col + _TQ] = part_t[:, blk * _TQ:(blk + 1) * _TQ]
    zk = zi[:, 2 * LANES: 3 * LANES]
    ms = jnp.sum(zk * zk, axis=-1, keepdims=True) * (1.0 / IDX_DIM)
    ki = _rope(zk * lax.rsqrt(ms + NORM_EPS) * gki_ref[...], c64, s64, IDX_DIM // 2)
    ki_ref[...] = ki[:, :IDX_DIM]
    k1, k2, _ = _split3(ki)
    k1, k2 = k1.astype(F32), k2.astype(F32)
    kcat_ref[:, :LANES] = (k1 + pltpu.roll(k2, LANES // 2, 1)).astype(BF16)
    kcat_ref[:, LANES:] = k1.astype(BF16)
    wo = zi[:, 3 * LANES: 4 * LANES]
    wo_ref[...] = wo
    wt_ref[...] = wo.T[:_ROWS, :]


def _proj(x, g, shift, scale, tabs, wm, wi, gq, gk, gki, tm):
    n, d = x.shape
    nt = n // tm
    assert tm % _TQ == 0
    groups, r, _ = shift.shape
    per = nt // groups if r == 1 else nt
    row = lambda w: pl.BlockSpec((tm, w), lambda i: (i, 0))
    nb = groups if r == 1 else 1
    col = lambda rows, mult=1: pl.BlockSpec((None, rows, tm * mult), lambda i: (i // (nt // nb), 0, i % (nt // nb)))
    full = lambda a: pl.BlockSpec(a.shape, lambda i: (0,) * a.ndim)
    mod = pl.BlockSpec((None, r, d), lambda i: (i // per, 0, 0))
    c64, s64, c128, s128 = tabs
    outs = [(ATTN_W, BF16), (ATTN_W, F32), (ATTN_W, BF16), (ATTN_W, F32),
            (IDX_HEADS * IDX_DIM, F32), (IDX_DIM, F32), (2 * LANES, BF16), (LANES, F32),
            (RET_W, BF16), (RET_W, BF16), (RET_W, BF16), (RET_W, F32)]
    outs_t = [(ATTN_W, 1, BF16), (ATTN_W, 1, F32), (ATTN_W, 1, F32), (ATTN_W, 1, BF16),
              (2 * LANES, IDX_HEADS, BF16), (_ROWS, 1, F32)]
    return pl.pallas_call(
        _proj_kernel,
        grid=(nt,),
        in_specs=[row(d), full(g), mod, mod, row(LANES), row(LANES), row(LANES), row(LANES),
                  full(wm), full(wi), full(gq), full(gk), full(gki)],
        out_specs=[row(w) for w, _ in outs] + [col(rows, mult) for rows, mult, _ in outs_t],
        out_shape=[jax.ShapeDtypeStruct((n, w), dt) for w, dt in outs]
        + [jax.ShapeDtypeStruct((nb, rows, n // nb * mult), dt) for rows, mult, dt in outs_t],
        compiler_params=pltpu.CompilerParams(vmem_limit_bytes=VMEM_LIMIT),
        name="proj",
    )(x, g, shift, scale, c64, s64, c128, s128, wm, wi, gq, gk, gki)


def _select_bias(s_ref, b_ref, n_chunks, kc, topk, query_pos):
    nq = s_ref.shape[1]
    fold = 64

    def partial_sums(x):
        parts = [x[j * fold:(j + 1) * fold] for j in range(kc // fold)]
        while len(parts) > 1:
            parts = [a + b for a, b in zip(parts[::2], parts[1::2])] + parts[len(parts) & ~1:]
        return parts[0]

    def column_sum(x):
        return jnp.sum(partial_sums(x), axis=0, keepdims=True)

    def count(pred):
        def body(c, cnt):
            off = pl.multiple_of(c * kc, kc)
            return cnt + partial_sums(jnp.where(pred(s_ref[pl.ds(off, kc), :]), 1.0, 0.0))
        cnt = lax.fori_loop(0, n_chunks, body, jnp.zeros((fold, nq), F32))
        return jnp.sum(cnt, axis=0, keepdims=True)

    def value_bit(it, p):
        cand = p | lax.shift_left(jnp.int32(1), 31 - it)
        t = cand ^ jnp.int32(INT_MIN)
        c = count(lambda s: s >= t)
        return jnp.where(c >= topk, cand, p)

    p = lax.fori_loop(0, 32, value_bit, jnp.zeros((1, nq), I32))
    t_star = p ^ jnp.int32(INT_MIN)
    need = topk - count(lambda s: s > t_star)

    upto = jnp.where(lax.broadcasted_iota(I32, (kc, kc), 1) <= lax.broadcasted_iota(I32, (kc, kc), 0),
                     1.0, 0.0).astype(BF16)

    def write(c, ties_before):
        off = pl.multiple_of(c * kc, kc)
        tie = jnp.where(s_ref[pl.ds(off, kc), :] == t_star, 1.0, 0.0)
        tie_b = tie.astype(BF16)
        for u in range(kc // _SUB):
            sub = pl.ds(pl.multiple_of(off + u * _SUB, _SUB), _SUB)
            s = s_ref[sub, :]
            key_pos = off + u * _SUB + lax.broadcasted_iota(I32, (_SUB, nq), 0)
            rank = ties_before + _dot(upto[u * _SUB:(u + 1) * _SUB], tie_b)
            tie_kept = jnp.where(rank <= need, s, t_star + 1) == t_star
            bias = jnp.where(s > t_star, 0.0, jnp.where(tie_kept, 0.0, NEG_BIG))
            b_ref[sub, :] = jnp.where(key_pos <= query_pos, bias, NEG_BIG)
        return ties_before + column_sum(tie)

    lax.fori_loop(0, n_chunks, write, jnp.zeros((1, nq), F32))


_SUB = 128


def _prompt_attend_kernel(qbt_ref, qcatt_ref, wt_ref, kcat_ref, kb_ref, vbt_ref, o_ref,
                          s_ref, b_ref, m_ref, l_ref, acc_ref, qh_ref, *, tq, kc, topk):
    i = pl.program_id(1)
    q0 = i * tq
    n_chunks = (q0 + tq + kc - 1) // kc
    query_pos = q0 + _lane_iota((1, tq))
    w = wt_ref[...] * (IDX_HEADS ** -0.5)

    def score_chunk(c, carry):
        for u in range(kc // _SUB):
            off = pl.multiple_of(c * kc + u * _SUB, _SUB)
            s = _dot(kcat_ref[pl.ds(off, _SUB), :], qcatt_ref[...])
            acc = None
            for hd in range(IDX_HEADS):
                term = jnp.maximum(s[:, hd * tq:(hd + 1) * tq], 0.0) * w[hd:hd + 1, :]
                acc = term if acc is None else acc + term
            key_pos = off + lax.broadcasted_iota(I32, (_SUB, tq), 0)
            s_ref[pl.ds(off, _SUB), :] = jnp.where(key_pos <= query_pos, _sortable(acc + 0.0),
                                                   jnp.int32(INT_MIN))
        return carry

    lax.fori_loop(0, n_chunks, score_chunk, 0)
    _select_bias(s_ref, b_ref, n_chunks, kc, topk, query_pos)

    m_ref[...] = jnp.full(m_ref.shape, NEG_BIG, F32)
    l_ref[...] = jnp.zeros(l_ref.shape, F32)
    acc_ref[...] = jnp.zeros(acc_ref.shape, F32)
    feat = lax.broadcasted_iota(I32, (LANES, tq), 0)
    for hd in range(N_HEADS):
        qp = qbt_ref[hd // 2 * LANES:(hd // 2 + 1) * LANES, :]
        qh_ref[hd] = jnp.where((feat < HEAD_DIM) == (hd % 2 == 0), qp, jnp.zeros_like(qp))

    def attend_chunk(c, carry):
        off = pl.multiple_of(c * kc, kc)
        for hd in range(N_HEADS):
            pair = slice(hd // 2 * LANES, (hd // 2 + 1) * LANES)
            qh = qh_ref[hd]
            m, l, acc = m_ref[hd], l_ref[hd], acc_ref[hd]
            for u in range(kc // _SUB):
                ks = pl.ds(pl.multiple_of(off + u * _SUB, _SUB), _SUB)
                s = _dot(kb_ref[ks, pair], qh) + b_ref[ks, :]
                m_new = jnp.maximum(m, jnp.max(s, axis=0, keepdims=True))
                a = jnp.exp2(m - m_new)
                p = jnp.exp2(s - m_new)
                l = a * l + jnp.sum(p, axis=0, keepdims=True)
                acc = a * acc + _dot(vbt_ref[hd * HEAD_DIM:(hd + 1) * HEAD_DIM, ks], p.astype(BF16))
                m = m_new
            m_ref[hd], l_ref[hd], acc_ref[hd] = m, l, acc
        return carry

    lax.fori_loop(0, n_chunks, attend_chunk, 0)
    for hd in range(N_HEADS):
        o_ref[hd * HEAD_DIM:(hd + 1) * HEAD_DIM, :] = acc_ref[hd] / l_ref[hd]


def _prompt_attend(qbt, qcatt, wt, kcat, kb, vbt, batch, seq):
    tq, kc = _TQ, min(512, seq)
    topk = min(TOPK_MAX, seq // 4)
    nq = seq // tq
    col = lambda rows, mult=1: pl.BlockSpec((None, rows, tq * mult), lambda b, i: (b, 0, i))
    res = lambda w: pl.BlockSpec((seq, w), lambda b, i: (b, 0), pipeline_mode=pl.Buffered(1))
    res_t = pl.BlockSpec((None, ATTN_W, seq), lambda b, i: (b, 0, 0), pipeline_mode=pl.Buffered(1))
    return pl.pallas_call(
        functools.partial(_prompt_attend_kernel, tq=tq, kc=kc, topk=topk),
        grid=(batch, nq),
        in_specs=[col(ATTN_W), col(qcatt.shape[1], IDX_HEADS), col(wt.shape[1]),
                  res(kcat.shape[1]), res(ATTN_W), res_t],
        out_specs=col(ATTN_W),
        out_shape=jax.ShapeDtypeStruct((batch, ATTN_W, seq), F32),
        scratch_shapes=[pltpu.VMEM((seq, tq), I32), pltpu.VMEM((seq, tq), F32),
                        pltpu.VMEM((N_HEADS, 1, tq), F32), pltpu.VMEM((N_HEADS, 1, tq), F32),
                        pltpu.VMEM((N_HEADS, HEAD_DIM, tq), F32), pltpu.VMEM((N_HEADS, LANES, tq), BF16)],
        compiler_params=pltpu.CompilerParams(vmem_limit_bytes=VMEM_LIMIT,
                                             dimension_semantics=("arbitrary", "arbitrary")),
        name="prompt_attend",
    )(qbt, qcatt, wt, kcat, kb, vbt)


_IDX_PAGES = 32
_KV_PAGES = 32
_ROWS = 8


def _sample_select_kernel(pt_ref, q_ref, w_ref, kn_ref, *rest, n_pages, topk, idx_bits, tokens):
    pages = rest[:_IDX_PAGES]
    bias_ref, s_ref = rest[_IDX_PAGES:]
    g = pl.program_id(1)
    q1, q2, _ = _split3(q_ref[...] * (IDX_DIM ** -0.5))
    q_cat = jnp.concatenate([q1, q1, q2, jnp.zeros_like(q1)], axis=1)
    w = w_ref[...] * (IDX_HEADS ** -0.5)

    def scores(keys_t):
        k1, k2, _ = _split3(keys_t)
        s = _dot(q_cat, jnp.concatenate([k1, k2, k1, jnp.zeros_like(k1)], axis=0))
        acc = None
        for hd in range(IDX_HEADS):
            term = jnp.maximum(s[hd * _ROWS:(hd + 1) * _ROWS], 0.0) * w[:, hd:hd + 1]
            acc = term if acc is None else acc + term
        return _sortable(acc + 0.0)

    sc = scores(jnp.concatenate([pages[u][...] for u in range(_IDX_PAGES)], axis=1))
    row0 = pl.multiple_of(g * _IDX_PAGES, _IDX_PAGES)
    for t in range(tokens):
        s_ref[t, pl.ds(row0, _IDX_PAGES), :] = jnp.concatenate(
            [sc[t:t + 1, u * PAGE_SIZE:(u + 1) * PAGE_SIZE] for u in range(_IDX_PAGES)], axis=0)

    @pl.when(g == pl.num_programs(1) - 1)
    def _():
        shape = (tokens, n_pages + _ROWS, LANES)
        lane1 = _lane_iota((1, LANES))
        sn = scores(kn_ref[...])
        floor = jnp.full((_ROWS - 1, LANES), INT_MIN, I32)
        for t in range(tokens):
            new_row = jnp.where(lane1 <= t, sn[t:t + 1], jnp.int32(INT_MIN))
            s_ref[t, n_pages:, :] = jnp.concatenate([new_row, floor], axis=0)

        idx = lax.broadcasted_iota(I32, shape, 1) * LANES + _lane_iota(shape)

        def count(hit):
            c = jnp.sum(jnp.where(hit, 1.0, 0.0), axis=1, keepdims=True)
            return jnp.sum(c, axis=2, keepdims=True)

        def value_bit(it, p):
            cand = p | lax.shift_left(jnp.int32(1), 31 - it)
            c = count(s_ref[...] >= (cand ^ jnp.int32(INT_MIN)))
            return jnp.where(c >= topk, cand, p)

        p = lax.fori_loop(0, 32, value_bit, jnp.zeros((tokens, 1, 1), I32))
        t_star = p ^ jnp.int32(INT_MIN)
        need = topk - count(s_ref[...] > t_star)

        def index_bit(it, r):
            cand = r | lax.shift_left(jnp.int32(1), idx_bits - 1 - it)
            c = count(jnp.where(idx < cand, s_ref[...], jnp.int32(INT_MIN)) == t_star)
            return jnp.where(c < need, cand, r)

        j_star = lax.fori_loop(0, idx_bits, index_bit, jnp.zeros((tokens, 1, 1), I32))
        s = s_ref[...]
        tie = jnp.where(idx <= j_star, s, jnp.int32(INT_MIN)) == t_star
        bias = jnp.where(s > t_star, 0.0, jnp.where(tie, 0.0, NEG_BIG))
        bias_ref[...] = jnp.where(s == jnp.int32(INT_MIN), NEG_BIG, bias)


def _sample_select(page_table, layer, kidx_t, q32, w8, ki_new_t, tokens):
    bd, n_pages = page_table.shape
    past = n_pages * PAGE_SIZE
    topk = min(TOPK_MAX, (past + tokens) // 4)
    idx_bits = (past + LANES - 1).bit_length()
    steps = n_pages // _IDX_PAGES
    page = lambda u: pl.BlockSpec((None, None, IDX_DIM, PAGE_SIZE),
                                  lambda b, g, pt: (layer, pt[b, g * _IDX_PAGES + u], 0, 0))
    per_b = lambda *blk: pl.BlockSpec((None,) + blk, lambda b, g, pt: (b,) + (0,) * len(blk))
    plane = (tokens, n_pages + _ROWS, LANES)
    grid_spec = pltpu.PrefetchScalarGridSpec(
        num_scalar_prefetch=1,
        grid=(bd, steps),
        in_specs=[per_b(IDX_HEADS * _ROWS, IDX_DIM), per_b(_ROWS, LANES), per_b(IDX_DIM, LANES)]
        + [page(u) for u in range(_IDX_PAGES)],
        out_specs=per_b(*plane),
        scratch_shapes=[pltpu.VMEM(plane, I32)],
    )
    return pl.pallas_call(
        functools.partial(_sample_select_kernel, n_pages=n_pages, topk=topk, idx_bits=idx_bits, tokens=tokens),
        grid_spec=grid_spec,
        out_shape=jax.ShapeDtypeStruct((bd,) + plane, F32),
        compiler_params=pltpu.CompilerParams(vmem_limit_bytes=VMEM_LIMIT,
                                             dimension_semantics=("arbitrary", "arbitrary")),
        name="sample_select",
    )(page_table, q32, w8, ki_new_t, *([kidx_t] * _IDX_PAGES))


def _sample_attend_kernel(pt_ref, q_ref, bias_ref, kn_ref, vn_ref, *rest, past, groups):
    kpages = rest[:_KV_PAGES]
    vpages = rest[_KV_PAGES:2 * _KV_PAGES]
    o_ref, s_ref, acc_ref = rest[2 * _KV_PAGES:]
    g = pl.program_id(1)
    width = _KV_PAGES * PAGE_SIZE

    @pl.when(g < groups)
    def _():
        keys_t = jnp.concatenate([kpages[u][...] for u in range(_KV_PAGES)], axis=1)
        s_ref[:, pl.ds(pl.multiple_of(g * width, width), width)] = _dot(q_ref[...], keys_t)

    @pl.when(g == groups - 1)
    def _():
        s_ref[:, past:] = _dot(q_ref[...], kn_ref[...])
        bias = bias_ref[...]
        for hd in range(N_HEADS):
            rows = slice(hd * _ROWS, (hd + 1) * _ROWS)
            s = s_ref[rows, :] + bias
            p = jnp.exp2(s - jnp.max(s, axis=-1, keepdims=True))
            s_ref[rows, :] = p / jnp.sum(p, axis=-1, keepdims=True)
        acc_ref[...] = jnp.zeros(acc_ref.shape, F32)

    @pl.when(g >= groups)
    def _():
        vals_t = jnp.concatenate([vpages[u][...] for u in range(_KV_PAGES)], axis=1)
        p = s_ref[:, pl.ds(pl.multiple_of((g - groups) * width, width), width)]
        acc_ref[...] += _dot_nt(p, vals_t)

    @pl.when(g == 2 * groups - 1)
    def _():
        acc = acc_ref[...] + _dot_nt(s_ref[:, past:], vn_ref[...])
        lane_head = _lane_iota((_ROWS, ATTN_W)) // HEAD_DIM
        out = jnp.zeros((_ROWS, ATTN_W), F32)
        for hd in range(N_HEADS):
            out = out + jnp.where(lane_head == hd, acc[hd * _ROWS:(hd + 1) * _ROWS], 0.0)
        o_ref[...] = out


def _sample_attend(page_table, layer, k_t, v_t, q_bd, bias, k_new_t, v_new_t):
    bd, n_pages = page_table.shape
    past = n_pages * PAGE_SIZE
    groups = n_pages // _KV_PAGES
    kpage = lambda u: pl.BlockSpec(
        (None, None, ATTN_W, PAGE_SIZE),
        lambda b, g, pt: (layer, pt[b, jnp.minimum(g, groups - 1) * _KV_PAGES + u], 0, 0))
    vpage = lambda u: pl.BlockSpec(
        (None, None, ATTN_W, PAGE_SIZE),
        lambda b, g, pt: (layer, pt[b, jnp.maximum(g - groups, 0) * _KV_PAGES + u], 0, 0))
    per_b = lambda r, w: pl.BlockSpec((None, r, w), lambda b, g, pt: (b, 0, 0))
    rows = N_HEADS * _ROWS
    grid_spec = pltpu.PrefetchScalarGridSpec(
        num_scalar_prefetch=1,
        grid=(bd, 2 * groups),
        in_specs=[per_b(rows, ATTN_W), per_b(_ROWS, past + LANES), per_b(ATTN_W, LANES), per_b(ATTN_W, LANES)]
        + [kpage(u) for u in range(_KV_PAGES)] + [vpage(u) for u in range(_KV_PAGES)],
        out_specs=per_b(_ROWS, ATTN_W),
        scratch_shapes=[pltpu.VMEM((rows, past + LANES), F32), pltpu.VMEM((rows, ATTN_W), F32)],
    )
    return pl.pallas_call(
        functools.partial(_sample_attend_kernel, past=past, groups=groups),
        grid_spec=grid_spec,
        out_shape=jax.ShapeDtypeStruct((bd, _ROWS, ATTN_W), F32),
        compiler_params=pltpu.CompilerParams(vmem_limit_bytes=VMEM_LIMIT,
                                             dimension_semantics=("arbitrary", "arbitrary")),
        name="sample_attend",
    )(page_table, q_bd, bias, k_new_t, v_new_t, *([k_t] * _KV_PAGES), *([v_t] * _KV_PAGES))


def _ret_kernel(q_ref, k_ref, v_ref, s0_ref, o_ref, sout_ref, state_ref, *, chunk, true_len):
    c = pl.program_id(1)

    @pl.when(c == 0)
    def _():
        state_ref[...] = s0_ref[...]

    ti = lax.broadcasted_iota(I32, (chunk, chunk), 0)
    tj = lax.broadcasted_iota(I32, (chunk, chunk), 1)
    diff = (ti - tj).astype(F32)
    t = lax.broadcasted_iota(I32, (chunk, 1), 0).astype(F32)
    for hd in range(RET_HEADS):
        sl = slice(hd * RET_DK, (hd + 1) * RET_DK)
        log_g = math.log(1.0 - 2.0 ** (-5.0 - hd))
        decay = jnp.where(diff >= 0, jnp.exp(log_g * jnp.maximum(diff, 0.0)), 0.0)
        q_dec = jnp.exp(log_g * (t + 1.0))
        k_dec = jnp.exp(log_g * (true_len - 1.0 - t))
        c_dec = math.exp(log_g * true_len)
        q, k, v = q_ref[:, sl], k_ref[:, sl], v_ref[:, sl]
        s = state_ref[hd]
        att = _dot_nt(q, k) * decay
        o_ref[:, sl] = _dot(att.astype(BF16), v) + _dot(q, s.astype(BF16)) * q_dec
        kd = (k.astype(F32) * k_dec).astype(BF16)
        state_ref[hd] = s * c_dec + _dot_tn(kd, v)

    @pl.when(c == pl.num_programs(1) - 1)
    def _():
        sout_ref[...] = state_ref[...]


def _retention(rq, rk, rv, state0, chunk, true_len):
    b = state0.shape[0]
    n = rq.shape[0] // (b * chunk)
    blk = pl.BlockSpec((chunk, RET_W), lambda i, c: (i * n + c, 0))
    st = pl.BlockSpec((None, RET_HEADS, RET_DK, RET_DV), lambda i, c: (i, 0, 0, 0))
    return pl.pallas_call(
        functools.partial(_ret_kernel, chunk=chunk, true_len=true_len),
        grid=(b, n),
        in_specs=[blk, blk, blk, st],
        out_specs=[blk, st],
        out_shape=[jax.ShapeDtypeStruct(rq.shape, F32), jax.ShapeDtypeStruct(state0.shape, F32)],
        scratch_shapes=[pltpu.VMEM((RET_HEADS, RET_DK, RET_DV), F32)],
        compiler_params=pltpu.CompilerParams(vmem_limit_bytes=VMEM_LIMIT,
                                             dimension_semantics=("arbitrary", "arbitrary")),
        name="retention",
    )(rq, rk, rv, state0)


def _mix_kernel(x_ref, g_ref, sh_ref, sc_ref, gt_ref, attn_ref, ret_ref, rg_ref, gret_ref,
                wga_ref, wgr_ref, wa_ref, wr_ref, wo_ref, o_ref):
    x = x_ref[...]
    hb = _norm_mod(x, g_ref[...], sh_ref[...], sc_ref[...]).astype(BF16)
    ret = ret_ref[...]
    rg = rg_ref[...]
    parts = []
    for hd in range(RET_HEADS):
        sl = slice(hd * RET_DV, (hd + 1) * RET_DV)
        r = ret[:, sl]
        ms = jnp.mean(r * r, axis=-1, keepdims=True)
        gate = rg[:, sl]
        parts.append(r * lax.rsqrt(ms + NORM_EPS) * gret_ref[:, sl] * (gate * jax.nn.sigmoid(gate)))
    retn = jnp.concatenate(parts, axis=-1).astype(BF16)
    y_a = _dot(attn_ref[...].astype(BF16), wa_ref[...])
    y_r = _dot(retn, wr_ref[...])
    ga = jax.nn.sigmoid(_dot(hb, wga_ref[...]))
    gr = jax.nn.sigmoid(_dot(hb, wgr_ref[...]))
    mix = _dot((ga * y_a + gr * y_r).astype(BF16), wo_ref[...])
    o_ref[...] = x + gt_ref[...] * mix


def _mix(x, g, shift, scale, gate, attn, ret, rg, gret, wga, wgr, wa, wr, wo, tm):
    n, d = x.shape
    nt = n // tm
    groups, r, _ = shift.shape
    per = nt // groups if r == 1 else nt
    row = lambda w: pl.BlockSpec((tm, w), lambda i: (i, 0))
    full = lambda a: pl.BlockSpec(a.shape, lambda i: (0,) * a.ndim)
    mod = pl.BlockSpec((None, r, d), lambda i: (i // per, 0, 0))
    return pl.pallas_call(
        _mix_kernel,
        grid=(nt,),
        in_specs=[row(d), full(g), mod, mod, mod, row(ATTN_W), row(RET_W), row(RET_W), full(gret),
                  full(wga), full(wgr), full(wa), full(wr), full(wo)],
        out_specs=row(d),
        out_shape=jax.ShapeDtypeStruct((n, d), F32),
        compiler_params=pltpu.CompilerParams(vmem_limit_bytes=VMEM_LIMIT),
        name="mix",
    )(x, g, shift, scale, gate, attn, ret, rg, gret, wga, wgr, wa, wr, wo)


_ROUTER_GROUP_LANE = N_EXPERTS


def _moe_kernel(x_ref, g_ref, sh_ref, sc_ref, gt_ref, wr_ref, br_ref, w1_ref, w3_ref, w2_ref,
                o_ref, act_ref):
    tm = x_ref.shape[0]
    x = x_ref[...]
    h = _norm_mod(x, g_ref[...], sh_ref[...], sc_ref[...])
    hb = h.astype(BF16)
    logits = _dot3(_split3(h), _split3(wr_ref[...])) + br_ref[...]
    lane_f = _lane_iota((tm, LANES)).astype(F32)

    def first_lane(hit):
        return jnp.min(jnp.where(hit, lane_f, float(LANES)), axis=-1, keepdims=True)

    is_group = jnp.abs(lane_f - (_ROUTER_GROUP_LANE + (N_GROUPS - 1) / 2)) < N_GROUPS / 2
    gl = jnp.where(is_group, logits, -jnp.inf)
    gmax = jnp.max(gl, axis=-1, keepdims=True)
    g_idx = first_lane(gl == gmax) - _ROUTER_GROUP_LANE
    g_w = 1.0 / jnp.sum(jnp.exp(gl - gmax), axis=-1, keepdims=True)
    group_mid = g_idx * EXPERTS_PER_GROUP + (EXPERTS_PER_GROUP - 1) / 2
    el = jnp.where(jnp.abs(lane_f - group_mid) < EXPERTS_PER_GROUP / 2, logits, -jnp.inf)
    m1 = jnp.max(el, axis=-1, keepdims=True)
    i1 = first_lane(el == m1)
    el2 = jnp.where(lane_f == i1, -jnp.inf, el)
    m2 = jnp.max(el2, axis=-1, keepdims=True)
    i2 = first_lane(el2 == m2)
    e2 = jnp.exp(m2 - m1)
    v1 = 1.0 / (1.0 + e2)
    v2 = e2 / (1.0 + e2)
    comb = jnp.where(lane_f == i1, g_w * v1, jnp.where(lane_f == i2, g_w * v2, 0.0))

    for e in range(N_EXPERTS):
        a = _dot(hb, w1_ref[e])
        b = _dot(hb, w3_ref[e])
        act_ref[:, e * D_EXPERT:(e + 1) * D_EXPERT] = ((a * jax.nn.sigmoid(a)) * b * comb[:, e:e + 1]).astype(BF16)
    o_ref[...] = x + gt_ref[...] * _dot(act_ref[...], w2_ref[...])


def _moe(x, g, shift, scale, gate, w_route, b_route, w1, w3, w2, tm):
    n, d = x.shape
    nt = n // tm
    groups, r, _ = shift.shape
    per = nt // groups if r == 1 else nt
    row = pl.BlockSpec((tm, d), lambda i: (i, 0))
    full = lambda a: pl.BlockSpec(a.shape, lambda i: (0,) * a.ndim)
    once = lambda a: pl.BlockSpec(a.shape, lambda i: (0,) * a.ndim, pipeline_mode=pl.Buffered(1))
    mod = pl.BlockSpec((None, r, d), lambda i: (i // per, 0, 0))
    return pl.pallas_call(
        _moe_kernel,
        grid=(nt,),
        in_specs=[row, full(g), mod, mod, mod, full(w_route), full(b_route), once(w1), once(w3), once(w2)],
        out_specs=row,
        out_shape=jax.ShapeDtypeStruct((n, d), F32),
        scratch_shapes=[pltpu.VMEM((tm, w2.shape[0]), BF16)],
        compiler_params=pltpu.CompilerParams(vmem_limit_bytes=VMEM_LIMIT),
        name="moe",
    )(x, g, shift, scale, gate, w_route, b_route, w1, w3, w2)


def _rope_tables(pos, dim):
    inv = ROPE_THETA ** (-jnp.arange(0, dim, 2, dtype=F32) / dim)
    ang = pos.astype(F32)[:, None] * inv[None, :]
    cos = jnp.concatenate([jnp.cos(ang), jnp.cos(ang)], axis=-1)
    sin = jnp.concatenate([-jnp.sin(ang), jnp.sin(ang)], axis=-1)
    rep = LANES // dim
    return jnp.tile(cos, (1, rep)), jnp.tile(sin, (1, rep))


def _pack_weights(l, w_in, g_q, g_k, g_kidx, w_group, b_group, w_router, b_router):
    offs = np.cumsum((ATTN_W, ATTN_W, ATTN_W, IDX_HEADS * IDX_DIM, IDX_DIM, IDX_HEADS,
                      RET_W, RET_W, RET_W, RET_W))
    w = w_in[l]
    d = w.shape[0]
    q, k, v, qi, ki, wi, rq, rk, rv, rg = (w[:, a:b] for a, b in zip((0, *offs[:-1]), offs))
    w_gate = w[:, offs[-1]:]
    wm = jnp.concatenate([q, k, v, rq, rk, rv, rg], axis=1).astype(BF16)
    zeros = lambda n: jnp.zeros((d, n), F32)
    widx = jnp.concatenate([qi, ki, zeros(LANES - IDX_DIM), wi, zeros(LANES - IDX_HEADS)], axis=1)
    half = w_gate.shape[1] // 2
    wga, wgr = w_gate[:, :half].astype(BF16), w_gate[:, half:].astype(BF16)
    gq = jnp.tile(g_q[l], LANES // HEAD_DIM)[None, :]
    gk = jnp.tile(g_k[l], LANES // HEAD_DIM)[None, :]
    gki = jnp.concatenate([g_kidx[l], jnp.zeros((LANES - IDX_DIM,), F32)])[None, :]
    pad = LANES - N_EXPERTS - N_GROUPS
    w_route = jnp.concatenate([w_router[l], w_group[l], zeros(pad)], axis=1)
    b_route = jnp.concatenate([b_router[l], b_group[l], jnp.zeros((pad,), F32)])[None, :]
    return wm, widx, wga, wgr, gq, gk, gki, w_route, b_route


def kernel(x_prompt, x_sample, cache_k, cache_v, cache_kidx, state_ret, page_table, c_prompt, c_sample,
           w_ada, b_ada, g_mix, g_ffn, w_in, g_q, g_k, g_kidx, g_ret, w_attn_out, w_ret_out, w_o,
           w_group, b_group, w_router, b_router, w_e1, w_e3, w_e2):
    bp, seq, d = x_prompt.shape
    bd, tokens, _ = x_sample.shape
    depth = w_in.shape[0]
    n_pages = page_table.shape[1]
    past = n_pages * PAGE_SIZE
    n_p, n_s = bp * seq, bd * tokens
    assert tokens <= _ROWS and n_pages % _IDX_PAGES == 0 and seq % RET_CHUNK == 0

    tabs_p = _rope_tables(jnp.arange(seq), HEAD_DIM) + _rope_tables(jnp.arange(seq), RET_DK)
    tabs_p = tuple(jnp.tile(t, (bp, 1)) for t in tabs_p)
    pos_s = past + jnp.arange(tokens)
    tabs_s = _rope_tables(pos_s, HEAD_DIM) + _rope_tables(pos_s, RET_DK)
    tabs_s = tuple(jnp.tile(t, (bd, 1)) for t in tabs_s)

    c_all = jnp.concatenate([c_prompt, c_sample], axis=0)
    c_rows = -(-c_all.shape[0] // 8) * 8
    c_all = jnp.pad(c_all, ((0, c_rows - c_all.shape[0]), (0, 0)))
    kidx_t = cache_kidx.transpose(0, 1, 3, 2)
    k_t = cache_k.transpose(0, 1, 3, 4, 2).reshape(cache_k.shape[:2] + (ATTN_W, PAGE_SIZE))
    v_t = cache_v.transpose(0, 1, 3, 4, 2).reshape(cache_v.shape[:2] + (ATTN_W, PAGE_SIZE))
    ret_pad = 16
    tm_p = 256 if seq % 256 == 0 else 128
    tm_moe = 512 if seq % 512 == 0 else tm_p

    y_p = x_prompt.reshape(n_p, d)
    y_s = x_sample.reshape(n_s, d)
    outs = {name: [] for name in ("kp", "vp", "kip", "sp", "ks", "vs", "kis", "ss")}
    for l in range(depth):
        wm, widx, wga, wgr, gq, gk, gki, w_route, b_route = _pack_weights(
            l, w_in, g_q, g_k, g_kidx, w_group, b_group, w_router, b_router)
        wa, wr, wo = w_attn_out[l].astype(BF16), w_ret_out[l].astype(BF16), w_o[l].astype(BF16)
        w1, w3 = w_e1[l].astype(BF16), w_e3[l].astype(BF16)
        w2 = w_e2[l].astype(BF16).reshape(N_EXPERTS * D_EXPERT, d)
        gmix, gffn, gret = g_mix[l][None, :], g_ffn[l][None, :], g_ret[l][None, :]

        ada = _ada(c_all, w_ada[l], b_ada[l])
        mods_p = [m[:bp, None, :] for m in jnp.split(ada, 6, axis=-1)]
        mods_s = [jnp.repeat(m[bp:bp + bd], tokens, axis=0)[None] for m in jnp.split(ada, 6, axis=-1)]

        sh_m, sc_m, gt_m, sh_f, sc_f, gt_f = mods_p
        (qb, k, kb, v, qi, ki, kcat, wi, rq, rk, rv, rg, qbt, kt, vt, vbt, qcatt, wt) = _proj(
            y_p, gmix, sh_m, sc_m, tabs_p, wm, widx, gq, gk, gki, tm_p)
        attn = _prompt_attend(qbt, qcatt, wt, kcat, kb, vbt, bp, seq).transpose(0, 2, 1).reshape(n_p, ATTN_W)
        ret, s_fin = _retention(rq, rk, rv, jnp.zeros((bp, RET_HEADS, RET_DK, RET_DV), F32),
                                RET_CHUNK, RET_CHUNK)
        y_p = _mix(y_p, gmix, sh_m, sc_m, gt_m, attn, ret, rg, gret, wga, wgr, wa, wr, wo, tm_p)
        y_p = _moe(y_p, gffn, sh_f, sc_f, gt_f, w_route, b_route, w1, w3, w2, tm_moe)
        by_token = lambda a: a.reshape(bp, N_HEADS, HEAD_DIM, seq).transpose(0, 3, 1, 2)
        outs["kp"].append(by_token(kt))
        outs["vp"].append(by_token(vt))
        outs["kip"].append(ki.reshape(bp, seq, IDX_DIM))
        outs["sp"].append(s_fin)

        sh_m, sc_m, gt_m, sh_f, sc_f, gt_f = mods_s
        (qb, k, kb, v, qi, ki, kcat, wi, rq, rk, rv, rg) = _proj(
            y_s, gmix, sh_m, sc_m, tabs_s, wm, widx, gq, gk, gki, n_s)[:12]
        pad_to = lambda a, rows: jnp.pad(a, ((0, 0), (0, rows - a.shape[1])) + ((0, 0),) * (a.ndim - 2))
        q32 = pad_to(qi.reshape(bd, tokens, IDX_HEADS, IDX_DIM), _ROWS).transpose(0, 2, 1, 3)
        q32 = q32.reshape(bd, IDX_HEADS * _ROWS, IDX_DIM)
        w8 = pad_to(wi.reshape(bd, tokens, LANES), _ROWS)
        new_t = lambda a: pad_to(a.reshape(bd, tokens, -1), LANES).transpose(0, 2, 1)
        plane = _sample_select(page_table, l, kidx_t, q32, w8, new_t(ki), tokens)
        bias = pad_to(plane[:, :, :n_pages + 1].reshape(bd, tokens, past + LANES), _ROWS)
        qh = pad_to(qb.astype(F32).reshape(bd, tokens, N_HEADS, HEAD_DIM), _ROWS).transpose(0, 2, 1, 3)
        q_bd = jnp.einsum('bhtd,hg->bhtgd', qh, jnp.eye(N_HEADS, dtype=F32)).reshape(bd, N_HEADS * _ROWS, ATTN_W)
        attn = _sample_attend(page_table, l, k_t, v_t, q_bd, bias, new_t(k), new_t(v))
        attn = attn[:, :tokens].reshape(n_s, ATTN_W)
        pad_chunk = lambda a: jnp.pad(a.reshape(bd, tokens, RET_W),
                                      ((0, 0), (0, ret_pad - tokens), (0, 0))).reshape(bd * ret_pad, RET_W)
        ret, s_fin = _retention(pad_chunk(rq), pad_chunk(rk), pad_chunk(rv), state_ret[l], ret_pad, tokens)
        ret = ret.reshape(bd, ret_pad, RET_W)[:, :tokens].reshape(n_s, RET_W)
        y_s = _mix(y_s, gmix, sh_m, sc_m, gt_m, attn, ret, rg, gret, wga, wgr, wa, wr, wo, n_s)
        y_s = _moe(y_s, gffn, sh_f, sc_f, gt_f, w_route, b_route, w1, w3, w2, n_s)
        outs["ks"].append(k.reshape(bd, tokens, N_HEADS, HEAD_DIM))
        outs["vs"].append(v.reshape(bd, tokens, N_HEADS, HEAD_DIM))
        outs["kis"].append(ki.reshape(bd, tokens, IDX_DIM))
        outs["ss"].append(s_fin)

    st = lambda name: jnp.stack(outs[name])
    return (y_p.reshape(bp, seq, d), y_s.reshape(bd, tokens, d),
            st("kp"), st("vp"), st("kip"), st("sp"), st("ks"), st("vs"), st("kis"), st("ss"))
```
